```python
import math
import jax, jax.numpy as jnp
from jax import lax
import numpy as np

D_MODEL = 4096
BATCH = 4
SEQ = 2048
DEPTH = 2

W_MIX = 2048
N_BRANCH = 3
A_HEADS = 16
A_DH = 64
A_QBLOCK = 128
B_CONV = 3
C_HEADS = 16
C_DH = W_MIX // C_HEADS
C_BLOCK = 256
C_TOPK = 3
C_QCHUNK = 16
N_EXPERTS = 64
TOP_K = 8
N_GROUPS = 8
TOPK_GROUPS = 4
D_EXPERT = 256
D_SHARED = 256
ROUTED_SCALE = 2.5
N_MOD = 6
EPS = 1e-6
N_IN = 9 * W_MIX + N_BRANCH * D_MODEL

kernel_name = "hybrid_gated_diffattn_shortconv_moba_moe"


def rmsnorm(x, g, eps=EPS):
    x32 = x.astype(jnp.float32)
    y = x32 * lax.rsqrt(jnp.mean(x32 * x32, axis=-1, keepdims=True) + eps)
    return (y * g.astype(jnp.float32)).astype(x.dtype)


def diff_attention(q, k, v, lam, lambda_init, subln_g):
    S = q.shape[3]
    scale = A_DH ** -0.5
    outs = []
    for i in range(S // A_QBLOCK):
        q0 = i * A_QBLOCK
        kend = q0 + A_QBLOCK
        s = jnp.einsum('bhmqd,bhmkd->bhmqk', q[:, :, :, q0:kend], k[:, :, :, :kend]).astype(jnp.float32) * scale
        causal = jnp.arange(kend)[None, :] <= (q0 + jnp.arange(A_QBLOCK))[:, None]
        p = jax.nn.softmax(jnp.where(causal, s, -jnp.inf), axis=-1)
        a = p[:, :, 0] - lam * p[:, :, 1]
        outs.append(jnp.einsum('bhqk,bhkd->bhqd', a.astype(v.dtype), v[:, :, :kend]))
    o = jnp.concatenate(outs, axis=2)
    return rmsnorm(o, subln_g, 1e-5) * (1.0 - lambda_init)


def moba_attention(q, k, v):
    B, H, S, dh = q.shape
    nb = -(-S // C_BLOCK)
    pad = nb * C_BLOCK - S
    kp = jnp.pad(k, ((0, 0), (0, 0), (0, pad), (0, 0)))
    vp = jnp.pad(v, ((0, 0), (0, 0), (0, pad), (0, 0)))
    kb = kp.reshape(B, H, nb, C_BLOCK, dh)
    vb = vp.reshape(B, H, nb, C_BLOCK, dh)
    kmean = jnp.mean(kb.astype(jnp.float32), axis=3)
    n_sel = min(C_TOPK, nb)
    scale = dh ** -0.5
    bi = jnp.arange(B)[:, None, None, None]
    hi = jnp.arange(H)[None, :, None, None]
    n_chunks = S // C_QCHUNK
    qc = jnp.moveaxis(q.reshape(B, H, n_chunks, C_QCHUNK, dh), 2, 0)

    def chunk(args):
        qi, ci = args
        start = ci * C_QCHUNK
        j = start // C_BLOCK
        gate = jnp.einsum('bhqd,bhnd->bhqn', qi.astype(jnp.float32), kmean)
        gate = jnp.where(jnp.arange(nb) < j, gate, -jnp.inf)
        _, idx = lax.top_k(gate, n_sel)
        valid = idx < j
        kg = kb[bi, hi, idx]
        vg = vb[bi, hi, idx]
        s_past = jnp.einsum('bhqd,bhqnjd->bhqnj', qi, kg).astype(jnp.float32) * scale
        s_past = jnp.where(valid[..., None], s_past, -jnp.inf)
        k_own = lax.dynamic_slice_in_dim(kp, j * C_BLOCK, C_BLOCK, axis=2)
        v_own = lax.dynamic_slice_in_dim(vp, j * C_BLOCK, C_BLOCK, axis=2)
        s_own = jnp.einsum('bhqd,bhjd->bhqj', qi, k_own).astype(jnp.float32) * scale
        qpos = start + jnp.arange(C_QCHUNK)
        kpos = j * C_BLOCK + jnp.arange(C_BLOCK)
        s_own = jnp.where(kpos[None, :] <= qpos[:, None], s_own, -jnp.inf)
        s = jnp.concatenate([s_past.reshape(B, H, C_QCHUNK, n_sel * C_BLOCK), s_own], axis=-1)
        p = jax.nn.softmax(s, axis=-1).astype(v.dtype)
        p_past = p[..., :n_sel * C_BLOCK].reshape(B, H, C_QCHUNK, n_sel, C_BLOCK)
        p_own = p[..., n_sel * C_BLOCK:]
        return (jnp.einsum('bhqnj,bhqnjd->bhqd', p_past, vg)
                + jnp.einsum('bhqj,bhjd->bhqd', p_own, v_own))

    out = lax.map(chunk, (qc, jnp.arange(n_chunks)))
    return jnp.moveaxis(out, 0, 2).reshape(B, H, S, dh)


def short_conv(u, w):
    return lax.conv_general_dilated(
        u, w[:, None, :].astype(u.dtype), window_strides=(1,), padding=[(B_CONV - 1, 0)],
        dimension_numbers=('NWC', 'WIO', 'NWC'), feature_group_count=u.shape[-1])


def moe_ffn(h, router_w, router_bias, w1, w3, w2, ws1, ws3, ws2):
    T = h.shape[0]
    scores = jax.nn.sigmoid((h @ router_w).astype(jnp.float32))
    choice = scores + router_bias.astype(jnp.float32)
    grp = choice.reshape(T, N_GROUPS, N_EXPERTS // N_GROUPS)
    grp_score = jnp.sum(lax.top_k(grp, 2)[0], axis=-1)
    _, gidx = lax.top_k(grp_score, TOPK_GROUPS)
    gmask = jnp.any(gidx[..., None] == jnp.arange(N_GROUPS), axis=-2)
    emask = jnp.repeat(gmask, N_EXPERTS // N_GROUPS, axis=-1)
    _, eidx = lax.top_k(jnp.where(emask, choice, -jnp.inf), TOP_K)
    wsel = jnp.take_along_axis(scores, eidx, axis=-1)
    wsel = wsel / jnp.sum(wsel, axis=-1, keepdims=True) * ROUTED_SCALE
    gates = jnp.sum(jax.nn.one_hot(eidx, N_EXPERTS, dtype=jnp.float32) * wsel[..., None], axis=-2)
    a = jnp.einsum('td,edf->tef', h, w1)
    b = jnp.einsum('td,edf->tef', h, w3)
    act = jax.nn.silu(a) * b * gates[:, :, None].astype(h.dtype)
    routed = jnp.einsum('tef,efd->td', act, w2)
    shared = (jax.nn.silu(h @ ws1) * (h @ ws3)) @ ws2
    return routed + shared


def setup_inputs(seed: int = 0) -> dict:
    key = jax.random.key(seed)
    ks = jax.random.split(key, 24)
    f32 = jnp.float32

    def nrm(k, shape, scale):
        return jax.random.normal(k, shape, f32) * scale

    L, D, W = DEPTH, D_MODEL, W_MIX
    return {
        "x": nrm(ks[0], (BATCH, SEQ, D), 1.0),
        "c": nrm(ks[1], (BATCH, D), 1.0),
        "w_ada": nrm(ks[2], (D, N_MOD * D), 0.5 * D ** -0.5),
        "ada_table": nrm(ks[3], (L, N_MOD, D), 0.1),
        "norm1_g": 1.0 + nrm(ks[4], (L, D), 0.02),
        "norm2_g": 1.0 + nrm(ks[5], (L, D), 0.02),
        "w_in": nrm(ks[6], (L, D, N_IN), D ** -0.5),
        "diff_lambda": nrm(ks[7], (L, 4, A_DH), 0.1),
        "diff_subln_g": 1.0 + nrm(ks[8], (L, 2 * A_DH), 0.02),
        "conv_w": nrm(ks[9], (L, B_CONV, W), B_CONV ** -0.5),
        "w_branch": nrm(ks[10], (L, N_BRANCH, W, D), W ** -0.5),
        "w_o": nrm(ks[11], (L, D, D), D ** -0.5),
        "router_w": nrm(ks[12], (L, D, N_EXPERTS), D ** -0.5),
        "router_bias": nrm(ks[13], (L, N_EXPERTS), 0.01),
        "exp_w1": nrm(ks[14], (L, N_EXPERTS, D, D_EXPERT), D ** -0.5),
        "exp_w3": nrm(ks[15], (L, N_EXPERTS, D, D_EXPERT), D ** -0.5),
        "exp_w2": nrm(ks[16], (L, N_EXPERTS, D_EXPERT, D), D_EXPERT ** -0.5),
        "shared_w1": nrm(ks[17], (L, D, D_SHARED), D ** -0.5),
        "shared_w3": nrm(ks[18], (L, D, D_SHARED), D ** -0.5),
        "shared_w2": nrm(ks[19], (L, D_SHARED, D), D_SHARED ** -0.5),
        "final_g": 1.0 + nrm(ks[20], (D,), 0.02),
    }


def reference(x, c, w_ada, ada_table, norm1_g, norm2_g, w_in, diff_lambda, diff_subln_g, conv_w,
              w_branch, w_o, router_w, router_bias, exp_w1, exp_w3, exp_w2,
              shared_w1, shared_w3, shared_w2, final_g):
    B, S, D = x.shape
    W = W_MIX
    cond = jax.nn.silu(c) @ w_ada
    splits = [W * i for i in range(1, 10)]
    for l in range(DEPTH):
        mod = cond.reshape(B, N_MOD, D) + ada_table[l]
        sh1, sc1, g1, sh2, sc2, g2 = [mod[:, i][:, None, :] for i in range(N_MOD)]

        h = rmsnorm(x, norm1_g[l]) * (1 + sc1) + sh1
        proj = h @ w_in[l]
        a_q, a_k, a_v, b_b, b_c, b_h, c_q, c_k, c_v, gate_logits = jnp.split(proj, splits, axis=-1)

        lambda_init = 0.8 - 0.6 * math.exp(-0.3 * l)
        dl = diff_lambda[l].astype(jnp.float32)
        lam = jnp.exp(jnp.sum(dl[0] * dl[1])) - jnp.exp(jnp.sum(dl[2] * dl[3])) + lambda_init
        qa = a_q.reshape(B, S, A_HEADS, 2, A_DH).transpose(0, 2, 3, 1, 4)
        ka = a_k.reshape(B, S, A_HEADS, 2, A_DH).transpose(0, 2, 3, 1, 4)
        va = a_v.reshape(B, S, A_HEADS, 2 * A_DH).transpose(0, 2, 1, 3)
        out_a = diff_attention(qa, ka, va, lam, lambda_init, diff_subln_g[l])
        out_a = out_a.transpose(0, 2, 1, 3).reshape(B, S, W)

        out_b = b_b * short_conv(b_c * b_h, conv_w[l])

        qc_ = c_q.reshape(B, S, C_HEADS, C_DH).transpose(0, 2, 1, 3)
        kc_ = c_k.reshape(B, S, C_HEADS, C_DH).transpose(0, 2, 1, 3)
        vc_ = c_v.reshape(B, S, C_HEADS, C_DH).transpose(0, 2, 1, 3)
        out_c = moba_attention(qc_, kc_, vc_).transpose(0, 2, 1, 3).reshape(B, S, W)

        br = jnp.einsum('bsgc,gcd->bsgd', jnp.stack([out_a, out_b, out_c], axis=2), w_branch[l])
        gates = jax.nn.sigmoid(gate_logits.reshape(B, S, N_BRANCH, D))
        merged = jnp.sum(gates * br, axis=2)
        x = x + g1 * (merged @ w_o[l])

        h2 = rmsnorm(x, norm2_g[l]) * (1 + sc2) + sh2
        y = moe_ffn(h2.reshape(B * S, D), router_w[l], router_bias[l], exp_w1[l], exp_w3[l], exp_w2[l],
                    shared_w1[l], shared_w3[l], shared_w2[l]).reshape(B, S, D)
        x = x + g2 * y
    return rmsnorm(x, final_g)
```

```python
import functools
import math

import jax
import jax.numpy as jnp
from jax import lax
from jax.experimental import pallas as pl
from jax.experimental.pallas import tpu as pltpu

F32 = jnp.float32
BF16 = jnp.bfloat16

A_DH = 64
HEAD_W = 128
C_BLOCK = 256
C_TOPK = 3
N_GROUPS = 8
TOPK_GROUPS = 4
TOP_K = 8
ROUTED_SCALE = 2.5
N_MOD = 6
B_CONV = 3
EPS = 1e-6
SUBLN_EPS = 1e-5
NEG = -1e30

VMEM_LIMIT_BYTES = 56 * 1024 * 1024
LANES = 128
SUBLANES = 8


def _params(*sem):
    return pltpu.CompilerParams(dimension_semantics=sem, vmem_limit_bytes=VMEM_LIMIT_BYTES)


def _nt_dot(a, b):
    return lax.dot_general(a, b, (((1,), (1,)), ((), ())), preferred_element_type=F32)


def _cast_rows(src_ref, dst_ref, rows):
    n = src_ref.shape[0]
    rows = min(rows, n)

    def body(r, carry):
        sl = pl.ds(pl.multiple_of(r * rows, rows), rows)
        dst_ref[sl, :] = src_ref[sl, :].astype(dst_ref.dtype)
        return carry

    lax.fori_loop(0, n // rows, body, 0)


def _ada_kernel(c_ref, w_ref, t_ref, o_ref):
    c = c_ref[...]
    a = (c * jax.nn.sigmoid(c)).astype(BF16)
    acc = jnp.dot(a, w_ref[...].astype(BF16), preferred_element_type=F32)
    o_ref[...] = acc[None, :, :] + t_ref[...]


def ada_mod(c, w_ada, ada_table, *, tn=512):
    B, D = c.shape
    L = ada_table.shape[0]
    N = w_ada.shape[1]
    tn = min(tn, N)
    cp = jnp.zeros((SUBLANES, D), F32).at[:B].set(c)
    table = ada_table.reshape(L, 1, N)
    return pl.pallas_call(
        _ada_kernel,
        grid=(N // tn,),
        in_specs=[pl.BlockSpec((SUBLANES, D), lambda j: (0, 0)),
                  pl.BlockSpec((D, tn), lambda j: (0, j)),
                  pl.BlockSpec((L, 1, tn), lambda j: (0, 0, j))],
        out_specs=pl.BlockSpec((L, SUBLANES, tn), lambda j: (0, 0, j)),
        out_shape=jax.ShapeDtypeStruct((L, SUBLANES, N), F32),
        compiler_params=_params("arbitrary"),
        name="ada_mod",
    )(cp, w_ada, table)


def _modnorm(x, g, sc, sh):
    ms = jnp.mean(x * x, axis=-1, keepdims=True)
    y = x * lax.rsqrt(ms + EPS) * g
    return y * (1.0 + sc) + sh


def _modnorm_kernel(x_ref, g_ref, sc_ref, sh_ref, o_ref):
    o_ref[...] = _modnorm(x_ref[...], g_ref[...], sc_ref[...], sh_ref[...]).astype(o_ref.dtype)


def modnorm(x, g, sc, sh, *, ts=256):
    B, S, D = x.shape
    ts = min(ts, S)
    row = pl.BlockSpec((None, ts, D), lambda b, i: (b, i, 0))
    per_b = pl.BlockSpec((None, 1, D), lambda b, i: (b, 0, 0))
    return pl.pallas_call(
        _modnorm_kernel,
        grid=(B, S // ts),
        in_specs=[row, pl.BlockSpec((1, D), lambda b, i: (0, 0)), per_b, per_b],
        out_specs=row,
        out_shape=jax.ShapeDtypeStruct((B, S, D), BF16),
        compiler_params=_params("arbitrary", "arbitrary"),
        name="modnorm",
    )(x, g.reshape(1, D), sc.reshape(B, 1, D), sh.reshape(B, 1, D))


def _mm_kernel(x_ref, w_ref, o_ref, wbf_ref):
    @pl.when(pl.program_id(1) == 0)
    def _():
        _cast_rows(w_ref, wbf_ref, 256)

    o_ref[...] = jnp.dot(x_ref[...], wbf_ref[...], preferred_element_type=F32).astype(o_ref.dtype)


def matmul_wcast(x, w, *, tm=1024, tn=512, out_dtype=BF16):
    M, K = x.shape
    N = w.shape[1]
    tm, tn = min(tm, M), min(tn, N)
    return pl.pallas_call(
        _mm_kernel,
        grid=(N // tn, M // tm),
        in_specs=[pl.BlockSpec((tm, K), lambda j, i: (i, 0)),
                  pl.BlockSpec((K, tn), lambda j, i: (0, j))],
        out_specs=pl.BlockSpec((tm, tn), lambda j, i: (i, j)),
        out_shape=jax.ShapeDtypeStruct((M, N), out_dtype),
        scratch_shapes=[pltpu.VMEM((K, tn), BF16)],
        compiler_params=_params("arbitrary", "arbitrary"),
        name="matmul_wcast",
    )(x, w)


def _mm_res_kernel(a_ref, w_ref, x_ref, g_ref, o_ref, wbf_ref):
    @pl.when(pl.program_id(1) == 0)
    def _():
        _cast_rows(w_ref, wbf_ref, 256)

    acc = jnp.dot(a_ref[...], wbf_ref[...], preferred_element_type=F32)
    o_ref[...] = x_ref[...] + g_ref[...] * acc


def matmul_residual(a, w, x, gate, *, seq, tm=512, tn=512):
    M, K = a.shape
    N = w.shape[1]
    B = gate.shape[0]
    tm, tn = min(tm, seq), min(tn, N)
    tiles_per_b = seq // tm
    return pl.pallas_call(
        _mm_res_kernel,
        grid=(N // tn, M // tm),
        in_specs=[pl.BlockSpec((tm, K), lambda j, i: (i, 0)),
                  pl.BlockSpec((K, tn), lambda j, i: (0, j)),
                  pl.BlockSpec((tm, tn), lambda j, i: (i, j)),
                  pl.BlockSpec((None, 1, tn), lambda j, i: (i // tiles_per_b, 0, j))],
        out_specs=pl.BlockSpec((tm, tn), lambda j, i: (i, j)),
        out_shape=jax.ShapeDtypeStruct((M, N), F32),
        scratch_shapes=[pltpu.VMEM((K, tn), BF16)],
        compiler_params=_params("arbitrary", "arbitrary"),
        name="matmul_residual",
    )(a, w, x, gate.reshape(B, 1, N))


def _diffattn_kernel(dl_ref, g_ref, q_ref, k_ref, v_ref, o_ref, *, tq, lambda_init):
    i = pl.program_id(2)
    dl = dl_ref[...]
    lam = (jnp.exp(jnp.sum(dl[0:1] * dl[1:2], keepdims=True))
           - jnp.exp(jnp.sum(dl[2:3] * dl[3:4], keepdims=True)) + lambda_init)

    q = q_ref[...] * jnp.asarray(A_DH ** -0.5, BF16)
    lane = lax.broadcasted_iota(jnp.int32, q.shape, 1)
    zero = jnp.zeros_like(q)
    qq = jnp.concatenate([jnp.where(lane < A_DH, q, zero), jnp.where(lane >= A_DH, q, zero)], axis=0)

    def attend(s, vj, m, l, acc):
        m_new = jnp.maximum(m, jnp.max(s, axis=-1, keepdims=True))
        alpha = jnp.exp(m - m_new)
        p = jnp.exp(s - m_new)
        l = alpha * l + jnp.sum(p, axis=-1, keepdims=True)
        acc = alpha * acc + jnp.dot(p.astype(BF16), vj, preferred_element_type=F32)
        return m_new, l, acc

    def body(j, carry):
        sl = pl.ds(pl.multiple_of(j * tq, tq), tq)
        return attend(_nt_dot(qq, k_ref[sl, :]), v_ref[sl, :], *carry)

    init = (jnp.full((2 * tq, 1), -jnp.inf, F32), jnp.zeros((2 * tq, 1), F32),
            jnp.zeros((2 * tq, 2 * A_DH), F32))
    carry = lax.fori_loop(0, i, body, init)

    sl = pl.ds(pl.multiple_of(i * tq, tq), tq)
    s = _nt_dot(qq, k_ref[sl, :])
    r = lax.broadcasted_iota(jnp.int32, s.shape, 0)
    r = jnp.where(r >= tq, r - tq, r)
    c = lax.broadcasted_iota(jnp.int32, s.shape, 1)
    s = jnp.where(c <= r, s, -jnp.inf)
    _, l, acc = attend(s, v_ref[sl, :], *carry)

    o = acc / l
    a = o[:tq] - lam * o[tq:]
    ms = jnp.mean(a * a, axis=-1, keepdims=True)
    y = a * lax.rsqrt(ms + SUBLN_EPS) * g_ref[...]
    o_ref[...] = (y * (1.0 - lambda_init)).astype(o_ref.dtype)


def diff_attention(proj, diff_lambda, subln_g, *, batch, seq, heads, col0, lambda_init, tq=256):
    tq = min(tq, seq)
    nq = seq // tq
    kern = functools.partial(_diffattn_kernel, tq=tq, lambda_init=lambda_init)
    return pl.pallas_call(
        kern,
        grid=(batch, heads, nq),
        in_specs=[pl.BlockSpec((4, A_DH), lambda b, h, i: (0, 0)),
                  pl.BlockSpec((1, HEAD_W), lambda b, h, i: (0, 0)),
                  pl.BlockSpec((tq, HEAD_W), lambda b, h, i: (b * nq + i, col0 + h)),
                  pl.BlockSpec((seq, HEAD_W), lambda b, h, i: (b, col0 + heads + h)),
                  pl.BlockSpec((seq, HEAD_W), lambda b, h, i: (b, col0 + 2 * heads + h))],
        out_specs=pl.BlockSpec((tq, HEAD_W), lambda b, h, i: (b * nq + i, h)),
        out_shape=jax.ShapeDtypeStruct((batch * seq, heads * HEAD_W), BF16),
        compiler_params=_params("arbitrary", "arbitrary", "arbitrary"),
        name="diff_attention",
    )(diff_lambda, subln_g.reshape(1, HEAD_W), proj, proj, proj)


def _conv_kernel(b_ref, c_ref, h_ref, w_ref, o_ref):
    u = c_ref[...].astype(F32) * h_ref[...].astype(F32)
    row = lax.broadcasted_iota(jnp.int32, u.shape, 0)
    w = w_ref[...]
    acc = w[B_CONV - 1:B_CONV] * u
    for d in range(1, B_CONV):
        shifted = jnp.where(row >= d, pltpu.roll(u, d, axis=0), 0.0)
        acc = acc + w[B_CONV - 1 - d:B_CONV - d] * shifted
    o_ref[...] = (b_ref[...].astype(F32) * acc).astype(o_ref.dtype)


def short_conv_gate(proj, conv_w, *, batch, seq, width, col0, cw=256):
    cw = min(cw, width)
    nc = width // cw
    c0 = col0 // cw

    def col(k):
        return pl.BlockSpec((seq, cw), lambda b, j: (b, c0 + k * nc + j))

    return pl.pallas_call(
        _conv_kernel,
        grid=(batch, nc),
        in_specs=[col(0), col(1), col(2), pl.BlockSpec((B_CONV, cw), lambda b, j: (0, j))],
        out_specs=pl.BlockSpec((seq, cw), lambda b, j: (b, j)),
        out_shape=jax.ShapeDtypeStruct((batch * seq, width), BF16),
        compiler_params=_params("arbitrary", "arbitrary"),
        name="short_conv_gate",
    )(proj, proj, proj, conv_w)


def _moba_kernel(q_ref, k_ref, v_ref, o_ref, km_ref, vt_ref, bias_ref, *, nb):
    i = pl.program_id(2)
    blk = C_BLOCK
    nbp = km_ref.shape[0]
    scale = HEAD_W ** -0.5

    @pl.when(i == 0)
    def _():
        km_ref[...] = jnp.zeros_like(km_ref)
        for n in range(nb):
            rows = slice(n * blk, (n + 1) * blk)
            km_ref[n:n + 1, :] = jnp.mean(k_ref[rows, :].astype(F32), axis=0, keepdims=True)
            vt_ref[n] = v_ref[rows, :].astype(F32).T.astype(BF16)

    q = q_ref[...]

    km = km_ref[...]
    km_hi = km.astype(BF16)
    km_lo = (km - km_hi.astype(F32)).astype(BF16)
    gate = _nt_dot(km_hi, q) + _nt_dot(km_lo, q)

    rowid = lax.broadcasted_iota(jnp.int32, gate.shape, 0)
    cnt = jnp.zeros(gate.shape, jnp.int32)
    for m in range(nb):
        gm = gate[m:m + 1, :]
        beats = jnp.where(gm > gate, 1, jnp.where((gm == gate) & (m < rowid), 1, 0))
        cnt = cnt + beats * (m < i).astype(jnp.int32)
    sel = (rowid < i) & (cnt < C_TOPK)
    bias = jnp.where(sel, 0.0, NEG)
    for n in range(nbp):
        bias_ref[n] = bias[n:n + 1, :]

    def attend(s, vt, m, l, acc):
        m_new = jnp.maximum(m, jnp.max(s, axis=0, keepdims=True))
        alpha = jnp.exp(m - m_new)
        p = jnp.exp(s - m_new)
        l = alpha * l + jnp.sum(p, axis=0, keepdims=True)
        acc = alpha * acc + jnp.dot(vt, p.astype(BF16), preferred_element_type=F32)
        return m_new, l, acc

    sl = pl.ds(pl.multiple_of(i * blk, blk), blk)
    s = _nt_dot(k_ref[sl, :], q) * scale
    kpos = lax.broadcasted_iota(jnp.int32, s.shape, 0)
    qpos = lax.broadcasted_iota(jnp.int32, s.shape, 1)
    s = jnp.where(kpos <= qpos, s, NEG)
    m0 = jnp.max(s, axis=0, keepdims=True)
    p0 = jnp.exp(s - m0)
    carry = (m0, jnp.sum(p0, axis=0, keepdims=True),
             jnp.dot(vt_ref[i], p0.astype(BF16), preferred_element_type=F32))

    def body(j, carry):
        sl = pl.ds(pl.multiple_of(j * blk, blk), blk)
        s = _nt_dot(k_ref[sl, :], q) * scale + bias_ref[j]
        return attend(s, vt_ref[j], *carry)

    _, l, acc = lax.fori_loop(0, i, body, carry)
    o_ref[...] = (acc / l).T.astype(o_ref.dtype)


def moba_attention(proj, *, batch, seq, heads, col0):
    nb = seq // C_BLOCK
    nbp = max(16, nb)
    kern = functools.partial(_moba_kernel, nb=nb)
    return pl.pallas_call(
        kern,
        grid=(batch, heads, nb),
        in_specs=[pl.BlockSpec((C_BLOCK, HEAD_W), lambda b, h, i: (b * nb + i, col0 + h)),
                  pl.BlockSpec((seq, HEAD_W), lambda b, h, i: (b, col0 + heads + h)),
                  pl.BlockSpec((seq, HEAD_W), lambda b, h, i: (b, col0 + 2 * heads + h))],
        out_specs=pl.BlockSpec((C_BLOCK, HEAD_W), lambda b, h, i: (b * nb + i, h)),
        out_shape=jax.ShapeDtypeStruct((batch * seq, heads * HEAD_W), BF16),
        scratch_shapes=[pltpu.VMEM((nbp, HEAD_W), F32),
                        pltpu.VMEM((nb, HEAD_W, C_BLOCK), BF16),
                        pltpu.VMEM((nbp, 1, C_BLOCK), F32)],
        compiler_params=_params("arbitrary", "arbitrary", "arbitrary"),
        name="moba_attention",
    )(proj, proj, proj)


def _branch_kernel(a_ref, b_ref, c_ref, ga_ref, gb_ref, gc_ref, w_ref, o_ref, wbf_ref):
    @pl.when(pl.program_id(1) == 0)
    def _():
        for g in range(3):
            _cast_rows(w_ref.at[g], wbf_ref.at[g], 256)

    acc = None
    for g, (x_ref, gl_ref) in enumerate(((a_ref, ga_ref), (b_ref, gb_ref), (c_ref, gc_ref))):
        br = jnp.dot(x_ref[...], wbf_ref[g], preferred_element_type=F32)
        term = jax.nn.sigmoid(gl_ref[...].astype(F32)) * br
        acc = term if acc is None else acc + term
    o_ref[...] = acc.astype(o_ref.dtype)


def branch_merge(out_a, out_b, out_c, proj, w_branch, *, gate_col0, tm=512, tn=512):
    M, W = out_a.shape
    D = w_branch.shape[2]
    tm, tn = min(tm, M), min(tn, D)
    g0 = gate_col0 // tn
    gstride = D // tn
    x_spec = pl.BlockSpec((tm, W), lambda j, i: (i, 0))

    def gate_spec(g):
        return pl.BlockSpec((tm, tn), lambda j, i: (i, g0 + g * gstride + j))

    return pl.pallas_call(
        _branch_kernel,
        grid=(D // tn, M // tm),
        in_specs=[x_spec, x_spec, x_spec, gate_spec(0), gate_spec(1), gate_spec(2),
                  pl.BlockSpec((3, W, tn), lambda j, i: (0, 0, j))],
        out_specs=pl.BlockSpec((tm, tn), lambda j, i: (i, j)),
        out_shape=jax.ShapeDtypeStruct((M, D), BF16),
        scratch_shapes=[pltpu.VMEM((3, W, tn), BF16)],
        compiler_params=_params("arbitrary", "arbitrary"),
        name="branch_merge",
    )(out_a, out_b, out_c, proj, proj, proj, w_branch)


def _rank_rows(v, n):
    rowid = lax.broadcasted_iota(jnp.int32, v.shape, 0)
    cnt = jnp.zeros(v.shape, jnp.int32)
    for m in range(n):
        vm = v[m:m + 1, :]
        cnt = cnt + jnp.where(vm > v, 1, jnp.where((vm == v) & (m < rowid), 1, 0))
    return cnt


def _route(logits_t, bias):
    E = logits_t.shape[0]
    gsz = E // N_GROUPS
    scores = jax.nn.sigmoid(logits_t)
    choice = scores + bias
    ridx = lax.broadcasted_iota(jnp.int32, (gsz, logits_t.shape[1]), 0).astype(F32)
    gid = lax.broadcasted_iota(jnp.int32, (N_GROUPS, logits_t.shape[1]), 0)
    gscore = jnp.zeros((N_GROUPS, logits_t.shape[1]), F32)
    for g in range(N_GROUPS):
        cg = choice[g * gsz:(g + 1) * gsz, :]
        m1 = jnp.max(cg, axis=0, keepdims=True)
        first = jnp.min(jnp.where(cg == m1, ridx, float(gsz)), axis=0, keepdims=True)
        m2 = jnp.max(jnp.where(ridx == first, -jnp.inf, cg), axis=0, keepdims=True)
        gscore = jnp.where(gid == g, m1 + m2, gscore)
    gsel = _rank_rows(gscore, N_GROUPS) < TOPK_GROUPS
    masked = jnp.concatenate(
        [jnp.where(gsel[g:g + 1, :], choice[g * gsz:(g + 1) * gsz, :], -jnp.inf) for g in range(N_GROUPS)],
        axis=0)
    sel = _rank_rows(masked, E) < TOP_K
    w = jnp.where(sel, scores, 0.0)
    return w / jnp.sum(w, axis=0, keepdims=True) * ROUTED_SCALE


def _norm_route_kernel(x_ref, g_ref, sc_ref, sh_ref, rw_ref, rb_ref, h_ref, gates_ref):
    h = _modnorm(x_ref[...], g_ref[...], sc_ref[...], sh_ref[...]).astype(BF16)
    h_ref[...] = h
    gates = _route(_nt_dot(rw_ref[...], h), rb_ref[...])
    pad = jnp.zeros((gates_ref.shape[1] - gates.shape[0], gates.shape[1]), F32)
    gates_ref[...] = jnp.concatenate([gates, pad], axis=0).T


def norm_route(x, g, sc, sh, router_w, router_bias, *, ts=256):
    B, S, D = x.shape
    E = router_w.shape[1]
    ts = min(ts, S)
    row = pl.BlockSpec((None, ts, D), lambda b, i: (b, i, 0))
    per_b = pl.BlockSpec((None, 1, D), lambda b, i: (b, 0, 0))
    return pl.pallas_call(
        _norm_route_kernel,
        grid=(B, S // ts),
        in_specs=[row, pl.BlockSpec((1, D), lambda b, i: (0, 0)), per_b, per_b,
                  pl.BlockSpec((E, D), lambda b, i: (0, 0)),
                  pl.BlockSpec((E, 1), lambda b, i: (0, 0))],
        out_specs=[row, pl.BlockSpec((None, ts, LANES), lambda b, i: (b, i, 0))],
        out_shape=[jax.ShapeDtypeStruct((B, S, D), BF16), jax.ShapeDtypeStruct((B, S, LANES), F32)],
        compiler_params=_params("arbitrary", "arbitrary"),
        name="norm_route",
    )(x, g.reshape(1, D), sc.reshape(B, 1, D), sh.reshape(B, 1, D),
      router_w.T.astype(BF16), router_bias.reshape(E, 1))


def _moe_dense_kernel(h_ref, g_ref, w1_ref, w3_ref, w2_ref, s1_ref, s3_ref, s2_ref, o_ref, *, n_exp):
    e = pl.program_id(1)
    h = h_ref[...]

    def ffn(w1, w3, w2, gate):
        a = jnp.dot(h, w1, preferred_element_type=F32)
        b = jnp.dot(h, w3, preferred_element_type=F32)
        act = a * jax.nn.sigmoid(a) * b
        if gate is not None:
            act = act * gate
        return jnp.dot(act.astype(BF16), w2, preferred_element_type=F32)

    @pl.when(e == 0)
    def _():
        o_ref[...] = jnp.zeros_like(o_ref)

    @pl.when(e < n_exp)
    def _():
        gt = g_ref[...]
        lane = lax.broadcasted_iota(jnp.int32, gt.shape, 1)
        gate = jnp.sum(jnp.where(lane == e, gt, 0.0), axis=1, keepdims=True)
        o_ref[...] += ffn(w1_ref[...], w3_ref[...], w2_ref[...], gate)

    @pl.when(e == n_exp)
    def _():
        o_ref[...] += ffn(s1_ref[...], s3_ref[...], s2_ref[...], None)


def moe_dense(h, gates, w1, w3, w2, s1, s3, s2, *, tm=512):
    T, D = h.shape
    E, _, Fe = w1.shape
    Fs = s1.shape[1]
    tm = min(tm, T)
    last = E - 1
    kern = functools.partial(_moe_dense_kernel, n_exp=E)
    return pl.pallas_call(
        kern,
        grid=(T // tm, E + 1),
        in_specs=[pl.BlockSpec((tm, D), lambda i, e: (i, 0)),
                  pl.BlockSpec((tm, LANES), lambda i, e: (i, 0)),
                  pl.BlockSpec((None, D, Fe), lambda i, e: (jnp.minimum(e, last), 0, 0)),
                  pl.BlockSpec((None, D, Fe), lambda i, e: (jnp.minimum(e, last), 0, 0)),
                  pl.BlockSpec((None, Fe, D), lambda i, e: (jnp.minimum(e, last), 0, 0)),
                  pl.BlockSpec((D, Fs), lambda i, e: (0, 0)),
                  pl.BlockSpec((D, Fs), lambda i, e: (0, 0)),
                  pl.BlockSpec((Fs, D), lambda i, e: (0, 0))],
        out_specs=pl.BlockSpec((tm, D), lambda i, e: (i, 0)),
        out_shape=jax.ShapeDtypeStruct((T, D), F32),
        compiler_params=_params("arbitrary", "arbitrary"),
        name="moe_dense",
    )(h, gates, w1, w3, w2, s1, s3, s2)


def _res_norm_kernel(x_ref, y_ref, gt_ref, g_ref, sc_ref, sh_ref, xo_ref, ho_ref):
    x = x_ref[...] + gt_ref[...] * y_ref[...]
    xo_ref[...] = x
    ho_ref[...] = _modnorm(x, g_ref[...], sc_ref[...], sh_ref[...]).astype(ho_ref.dtype)


def residual_norm(x, y, gate, g, sc, sh, *, out_dtype, ts=256):
    B, S, D = x.shape
    ts = min(ts, S)
    row = pl.BlockSpec((None, ts, D), lambda b, i: (b, i, 0))
    per_b = pl.BlockSpec((None, 1, D), lambda b, i: (b, 0, 0))
    return pl.pallas_call(
        _res_norm_kernel,
        grid=(B, S // ts),
        in_specs=[row, row, per_b, pl.BlockSpec((1, D), lambda b, i: (0, 0)), per_b, per_b],
        out_specs=[row, row],
        out_shape=[jax.ShapeDtypeStruct((B, S, D), F32), jax.ShapeDtypeStruct((B, S, D), out_dtype)],
        compiler_params=_params("arbitrary", "arbitrary"),
        name="residual_norm",
    )(x, y, gate.reshape(B, 1, D), g.reshape(1, D), sc.reshape(B, 1, D), sh.reshape(B, 1, D))


def kernel(x, c, w_ada, ada_table, norm1_g, norm2_g, w_in, diff_lambda, diff_subln_g, conv_w,
           w_branch, w_o, router_w, router_bias, exp_w1, exp_w3, exp_w2,
           shared_w1, shared_w3, shared_w2, final_g):
    B, S, D = x.shape
    L = w_in.shape[0]
    W = w_branch.shape[2]
    heads = W // HEAD_W
    T = B * S
    assert w_in.shape[2] == 9 * W + 3 * D and S % C_BLOCK == 0

    mod = ada_mod(c, w_ada, ada_table)

    def mod_of(l, k):
        return mod[l, :B, k * D:(k + 1) * D]

    zeros_bd = jnp.zeros((B, D), F32)
    h = modnorm(x, norm1_g[0], mod_of(0, 1), mod_of(0, 0))
    for l in range(L):
        proj = matmul_wcast(h.reshape(T, D), w_in[l])
        lambda_init = 0.8 - 0.6 * math.exp(-0.3 * l)
        out_a = diff_attention(proj, diff_lambda[l], diff_subln_g[l], batch=B, seq=S, heads=heads,
                               col0=0, lambda_init=lambda_init)
        out_b = short_conv_gate(proj, conv_w[l], batch=B, seq=S, width=W, col0=3 * W)
        out_c = moba_attention(proj, batch=B, seq=S, heads=heads, col0=6 * heads)
        merged = branch_merge(out_a, out_b, out_c, proj, w_branch[l], gate_col0=9 * W)
        x1 = matmul_residual(merged, w_o[l], x.reshape(T, D), mod_of(l, 2), seq=S).reshape(B, S, D)

        h2, gates = norm_route(x1, norm2_g[l], mod_of(l, 4), mod_of(l, 3), router_w[l], router_bias[l])
        y = moe_dense(h2.reshape(T, D), gates.reshape(T, LANES),
                      exp_w1[l].astype(BF16), exp_w3[l].astype(BF16), exp_w2[l].astype(BF16),
                      shared_w1[l].astype(BF16), shared_w3[l].astype(BF16), shared_w2[l].astype(BF16))
        if l + 1 < L:
            x, h = residual_norm(x1, y.reshape(B, S, D), mod_of(l, 5), norm1_g[l + 1],
                                 mod_of(l + 1, 1), mod_of(l + 1, 0), out_dtype=BF16)
        else:
            x, h = residual_norm(x1, y.reshape(B, S, D), mod_of(l, 5), final_g,
                                 zeros_bd, zeros_bd, out_dtype=F32)
    return h
```

```python
import functools
import math

import jax
import jax.numpy as jnp
from jax import lax
from jax.experimental import pallas as pl
from jax.experimental.pallas import tpu as pltpu

F32 = jnp.float32
BF16 = jnp.bfloat16

A_DH = 64
HEAD_W = 128
C_BLOCK = 256
C_TOPK = 3
N_GROUPS = 8
TOPK_GROUPS = 4
TOP_K = 8
ROUTED_SCALE = 2.5
N_MOD = 6
B_CONV = 3
EPS = 1e-6
SUBLN_EPS = 1e-5
NEG = -1e30

VMEM_LIMIT_BYTES = 56 * 1024 * 1024
LANES = 128
SUBLANES = 8
MOE_ROW_TILE = 256


def _params(*sem):
    return pltpu.CompilerParams(dimension_semantics=sem, vmem_limit_bytes=VMEM_LIMIT_BYTES)


def _nt_dot(a, b):
    return lax.dot_general(a, b, (((1,), (1,)), ((), ())), preferred_element_type=F32)


def _cast_rows(src_ref, dst_ref, rows):
    n = src_ref.shape[0]
    rows = min(rows, n)

    def body(r, carry):
        sl = pl.ds(pl.multiple_of(r * rows, rows), rows)
        dst_ref[sl, :] = src_ref[sl, :].astype(dst_ref.dtype)
        return carry

    lax.fori_loop(0, n // rows, body, 0)


def _ada_kernel(c_ref, w_ref, t_ref, o_ref):
    c = c_ref[...]
    a = (c * jax.nn.sigmoid(c)).astype(BF16)
    acc = jnp.dot(a, w_ref[...].astype(BF16), preferred_element_type=F32)
    o_ref[...] = acc[None, :, :] + t_ref[...]


def ada_mod(c, w_ada, ada_table, *, tn=512):
    B, D = c.shape
    L = ada_table.shape[0]
    N = w_ada.shape[1]
    tn = min(tn, N)
    cp = jnp.zeros((SUBLANES, D), F32).at[:B].set(c)
    table = ada_table.reshape(L, 1, N)
    return pl.pallas_call(
        _ada_kernel,
        grid=(N // tn,),
        in_specs=[pl.BlockSpec((SUBLANES, D), lambda j: (0, 0)),
                  pl.BlockSpec((D, tn), lambda j: (0, j)),
                  pl.BlockSpec((L, 1, tn), lambda j: (0, 0, j))],
        out_specs=pl.BlockSpec((L, SUBLANES, tn), lambda j: (0, 0, j)),
        out_shape=jax.ShapeDtypeStruct((L, SUBLANES, N), F32),
        compiler_params=_params("arbitrary"),
        name="ada_mod",
    )(cp, w_ada, table)


def _modnorm(x, g, sc, sh):
    ms = jnp.mean(x * x, axis=-1, keepdims=True)
    y = x * lax.rsqrt(ms + EPS) * g
    return y * (1.0 + sc) + sh


def _modnorm_kernel(x_ref, g_ref, sc_ref, sh_ref, o_ref):
    o_ref[...] = _modnorm(x_ref[...], g_ref[...], sc_ref[...], sh_ref[...]).astype(o_ref.dtype)


def modnorm(x, g, sc, sh, *, ts=256):
    B, S, D = x.shape
    ts = min(ts, S)
    row = pl.BlockSpec((None, ts, D), lambda b, i: (b, i, 0))
    per_b = pl.BlockSpec((None, 1, D), lambda b, i: (b, 0, 0))
    return pl.pallas_call(
        _modnorm_kernel,
        grid=(B, S // ts),
        in_specs=[row, pl.BlockSpec((1, D), lambda b, i: (0, 0)), per_b, per_b],
        out_specs=row,
        out_shape=jax.ShapeDtypeStruct((B, S, D), BF16),
        compiler_params=_params("arbitrary", "arbitrary"),
        name="modnorm",
    )(x, g.reshape(1, D), sc.reshape(B, 1, D), sh.reshape(B, 1, D))


def _mm_kernel(x_ref, w_ref, o_ref, wbf_ref):
    @pl.when(pl.program_id(1) == 0)
    def _():
        _cast_rows(w_ref, wbf_ref, 256)

    o_ref[...] = jnp.dot(x_ref[...], wbf_ref[...], preferred_element_type=F32).astype(o_ref.dtype)


def matmul_wcast(x, w, layer, *, tm=1024, tn=512, out_dtype=BF16):
    M, K = x.shape
    N = w.shape[2]
    tm, tn = min(tm, M), min(tn, N)
    return pl.pallas_call(
        _mm_kernel,
        grid=(N // tn, M // tm),
        in_specs=[pl.BlockSpec((tm, K), lambda j, i: (i, 0)),
                  pl.BlockSpec((None, K, tn), lambda j, i: (layer, 0, j))],
        out_specs=pl.BlockSpec((tm, tn), lambda j, i: (i, j)),
        out_shape=jax.ShapeDtypeStruct((M, N), out_dtype),
        scratch_shapes=[pltpu.VMEM((K, tn), BF16)],
        compiler_params=_params("arbitrary", "arbitrary"),
        name="matmul_wcast",
    )(x, w)


def _mm_res_kernel(a_ref, w_ref, x_ref, g_ref, o_ref, wbf_ref):
    @pl.when(pl.program_id(1) == 0)
    def _():
        _cast_rows(w_ref, wbf_ref, 256)

    acc = jnp.dot(a_ref[...], wbf_ref[...], preferred_element_type=F32)
    o_ref[...] = x_ref[...] + g_ref[...] * acc


def matmul_residual(a, w, layer, x, gate, *, seq, tm=512, tn=512):
    M, K = a.shape
    N = w.shape[2]
    B = gate.shape[0]
    tm, tn = min(tm, seq), min(tn, N)
    tiles_per_b = seq // tm
    return pl.pallas_call(
        _mm_res_kernel,
        grid=(N // tn, M // tm),
        in_specs=[pl.BlockSpec((tm, K), lambda j, i: (i, 0)),
                  pl.BlockSpec((None, K, tn), lambda j, i: (layer, 0, j)),
                  pl.BlockSpec((tm, tn), lambda j, i: (i, j)),
                  pl.BlockSpec((None, 1, tn), lambda j, i: (i // tiles_per_b, 0, j))],
        out_specs=pl.BlockSpec((tm, tn), lambda j, i: (i, j)),
        out_shape=jax.ShapeDtypeStruct((M, N), F32),
        scratch_shapes=[pltpu.VMEM((K, tn), BF16)],
        compiler_params=_params("arbitrary", "arbitrary"),
        name="matmul_residual",
    )(a, w, x, gate.reshape(B, 1, N))


def _diffattn_kernel(dl_ref, g_ref, q_ref, k_ref, v_ref, o_ref, *, tq, lambda_init):
    i = pl.program_id(2)
    dl = dl_ref[...]
    lam = (jnp.exp(jnp.sum(dl[0:1] * dl[1:2], keepdims=True))
           - jnp.exp(jnp.sum(dl[2:3] * dl[3:4], keepdims=True)) + lambda_init)

    q = q_ref[...] * jnp.asarray(A_DH ** -0.5, BF16)
    lane = lax.broadcasted_iota(jnp.int32, q.shape, 1)
    zero = jnp.zeros_like(q)
    qq = jnp.concatenate([jnp.where(lane < A_DH, q, zero), jnp.where(lane >= A_DH, q, zero)], axis=0)

    def attend(s, vj, m, l, acc):
        m_new = jnp.maximum(m, jnp.max(s, axis=-1, keepdims=True))
        alpha = jnp.exp(m - m_new)
        p = jnp.exp(s - m_new)
        l = alpha * l + jnp.sum(p, axis=-1, keepdims=True)
        acc = alpha * acc + jnp.dot(p.astype(BF16), vj, preferred_element_type=F32)
        return m_new, l, acc

    def body(j, carry):
        sl = pl.ds(pl.multiple_of(j * tq, tq), tq)
        return attend(_nt_dot(qq, k_ref[sl, :]), v_ref[sl, :], *carry)

    init = (jnp.full((2 * tq, 1), -jnp.inf, F32), jnp.zeros((2 * tq, 1), F32),
            jnp.zeros((2 * tq, 2 * A_DH), F32))
    carry = lax.fori_loop(0, i, body, init)

    sl = pl.ds(pl.multiple_of(i * tq, tq), tq)
    s = _nt_dot(qq, k_ref[sl, :])
    r = lax.broadcasted_iota(jnp.int32, s.shape, 0)
    r = jnp.where(r >= tq, r - tq, r)
    c = lax.broadcasted_iota(jnp.int32, s.shape, 1)
    s = jnp.where(c <= r, s, -jnp.inf)
    _, l, acc = attend(s, v_ref[sl, :], *carry)

    o = acc / l
    a = o[:tq] - lam * o[tq:]
    ms = jnp.mean(a * a, axis=-1, keepdims=True)
    y = a * lax.rsqrt(ms + SUBLN_EPS) * g_ref[...]
    o_ref[...] = (y * (1.0 - lambda_init)).astype(o_ref.dtype)


def diff_attention(proj, diff_lambda, subln_g, *, batch, seq, heads, col0, lambda_init, tq=256):
    tq = min(tq, seq)
    nq = seq // tq
    kern = functools.partial(_diffattn_kernel, tq=tq, lambda_init=lambda_init)
    return pl.pallas_call(
        kern,
        grid=(batch, heads, nq),
        in_specs=[pl.BlockSpec((4, A_DH), lambda b, h, i: (0, 0)),
                  pl.BlockSpec((1, HEAD_W), lambda b, h, i: (0, 0)),
                  pl.BlockSpec((tq, HEAD_W), lambda b, h, i: (b * nq + i, col0 + h)),
                  pl.BlockSpec((seq, HEAD_W), lambda b, h, i: (b, col0 + heads + h)),
                  pl.BlockSpec((seq, HEAD_W), lambda b, h, i: (b, col0 + 2 * heads + h))],
        out_specs=pl.BlockSpec((tq, HEAD_W), lambda b, h, i: (b * nq + i, h)),
        out_shape=jax.ShapeDtypeStruct((batch * seq, heads * HEAD_W), BF16),
        compiler_params=_params("arbitrary", "arbitrary", "arbitrary"),
        name="diff_attention",
    )(diff_lambda, subln_g.reshape(1, HEAD_W), proj, proj, proj)


def _conv_kernel(b_ref, c_ref, h_ref, w_ref, o_ref):
    u = c_ref[...].astype(F32) * h_ref[...].astype(F32)
    row = lax.broadcasted_iota(jnp.int32, u.shape, 0)
    w = w_ref[...]
    acc = w[B_CONV - 1:B_CONV] * u
    for d in range(1, B_CONV):
        shifted = jnp.where(row >= d, pltpu.roll(u, d, axis=0), 0.0)
        acc = acc + w[B_CONV - 1 - d:B_CONV - d] * shifted
    o_ref[...] = (b_ref[...].astype(F32) * acc).astype(o_ref.dtype)


def short_conv_gate(proj, conv_w, *, batch, seq, width, col0, cw=256):
    cw = min(cw, width)
    nc = width // cw
    c0 = col0 // cw

    def col(k):
        return pl.BlockSpec((seq, cw), lambda b, j: (b, c0 + k * nc + j))

    return pl.pallas_call(
        _conv_kernel,
        grid=(batch, nc),
        in_specs=[col(0), col(1), col(2), pl.BlockSpec((B_CONV, cw), lambda b, j: (0, j))],
        out_specs=pl.BlockSpec((seq, cw), lambda b, j: (b, j)),
        out_shape=jax.ShapeDtypeStruct((batch * seq, width), BF16),
        compiler_params=_params("arbitrary", "arbitrary"),
        name="short_conv_gate",
    )(proj, proj, proj, conv_w)


def _moba_kernel(q_ref, k_ref, v_ref, o_ref, km_ref, vt_ref, bias_ref, *, nb):
    i = pl.program_id(2)
    blk = C_BLOCK
    nbp = km_ref.shape[0]
    scale = HEAD_W ** -0.5

    @pl.when(i == 0)
    def _():
        km_ref[...] = jnp.zeros_like(km_ref)
        for n in range(nb):
            rows = slice(n * blk, (n + 1) * blk)
            km_ref[n:n + 1, :] = jnp.mean(k_ref[rows, :].astype(F32), axis=0, keepdims=True)
            vt_ref[n] = v_ref[rows, :].astype(F32).T.astype(BF16)

    q = q_ref[...]

    km = km_ref[...]
    km_hi = km.astype(BF16)
    km_lo = (km - km_hi.astype(F32)).astype(BF16)
    gate = _nt_dot(km_hi, q) + _nt_dot(km_lo, q)

    rowid = lax.broadcasted_iota(jnp.int32, gate.shape, 0)
    cnt = jnp.zeros(gate.shape, jnp.int32)
    for m in range(nb):
        gm = gate[m:m + 1, :]
        beats = jnp.where(gm > gate, 1, jnp.where((gm == gate) & (m < rowid), 1, 0))
        cnt = cnt + beats * (m < i).astype(jnp.int32)
    sel = (rowid < i) & (cnt < C_TOPK)
    bias = jnp.where(sel, 0.0, NEG)
    for n in range(nbp):
        bias_ref[n] = bias[n:n + 1, :]

    def attend(s, vt, m, l, acc):
        m_new = jnp.maximum(m, jnp.max(s, axis=0, keepdims=True))
        alpha = jnp.exp(m - m_new)
        p = jnp.exp(s - m_new)
        l = alpha * l + jnp.sum(p, axis=0, keepdims=True)
        acc = alpha * acc + jnp.dot(vt, p.astype(BF16), preferred_element_type=F32)
        return m_new, l, acc

    sl = pl.ds(pl.multiple_of(i * blk, blk), blk)
    s = _nt_dot(k_ref[sl, :], q) * scale
    kpos = lax.broadcasted_iota(jnp.int32, s.shape, 0)
    qpos = lax.broadcasted_iota(jnp.int32, s.shape, 1)
    s = jnp.where(kpos <= qpos, s, NEG)
    m0 = jnp.max(s, axis=0, keepdims=True)
    p0 = jnp.exp(s - m0)
    carry = (m0, jnp.sum(p0, axis=0, keepdims=True),
             jnp.dot(vt_ref[i], p0.astype(BF16), preferred_element_type=F32))

    def body(j, carry):
        sl = pl.ds(pl.multiple_of(j * blk, blk), blk)
        s = _nt_dot(k_ref[sl, :], q) * scale + bias_ref[j]
        return attend(s, vt_ref[j], *carry)

    _, l, acc = lax.fori_loop(0, i, body, carry)
    o_ref[...] = (acc / l).T.astype(o_ref.dtype)


def moba_attention(proj, *, batch, seq, heads, col0):
    nb = seq // C_BLOCK
    nbp = max(16, nb)
    kern = functools.partial(_moba_kernel, nb=nb)
    return pl.pallas_call(
        kern,
        grid=(batch, heads, nb),
        in_specs=[pl.BlockSpec((C_BLOCK, HEAD_W), lambda b, h, i: (b * nb + i, col0 + h)),
                  pl.BlockSpec((seq, HEAD_W), lambda b, h, i: (b, col0 + heads + h)),
                  pl.BlockSpec((seq, HEAD_W), lambda b, h, i: (b, col0 + 2 * heads + h))],
        out_specs=pl.BlockSpec((C_BLOCK, HEAD_W), lambda b, h, i: (b * nb + i, h)),
        out_shape=jax.ShapeDtypeStruct((batch * seq, heads * HEAD_W), BF16),
        scratch_shapes=[pltpu.VMEM((nbp, HEAD_W), F32),
                        pltpu.VMEM((nb, HEAD_W, C_BLOCK), BF16),
                        pltpu.VMEM((nbp, 1, C_BLOCK), F32)],
        compiler_params=_params("arbitrary", "arbitrary", "arbitrary"),
        name="moba_attention",
    )(proj, proj, proj)


def _branch_kernel(a_ref, b_ref, c_ref, ga_ref, gb_ref, gc_ref, w_ref, o_ref, wbf_ref):
    @pl.when(pl.program_id(1) == 0)
    def _():
        for g in range(3):
            _cast_rows(w_ref.at[g], wbf_ref.at[g], 256)

    acc = None
    for g, (x_ref, gl_ref) in enumerate(((a_ref, ga_ref), (b_ref, gb_ref), (c_ref, gc_ref))):
        br = jnp.dot(x_ref[...], wbf_ref[g], preferred_element_type=F32)
        term = jax.nn.sigmoid(gl_ref[...].astype(F32)) * br
        acc = term if acc is None else acc + term
    o_ref[...] = acc.astype(o_ref.dtype)


def branch_merge(out_a, out_b, out_c, proj, w_branch, layer, *, gate_col0, tm=512, tn=512):
    M, W = out_a.shape
    D = w_branch.shape[3]
    tm, tn = min(tm, M), min(tn, D)
    g0 = gate_col0 // tn
    gstride = D // tn
    x_spec = pl.BlockSpec((tm, W), lambda j, i: (i, 0))

    def gate_spec(g):
        return pl.BlockSpec((tm, tn), lambda j, i: (i, g0 + g * gstride + j))

    return pl.pallas_call(
        _branch_kernel,
        grid=(D // tn, M // tm),
        in_specs=[x_spec, x_spec, x_spec, gate_spec(0), gate_spec(1), gate_spec(2),
                  pl.BlockSpec((None, 3, W, tn), lambda j, i: (layer, 0, 0, j))],
        out_specs=pl.BlockSpec((tm, tn), lambda j, i: (i, j)),
        out_shape=jax.ShapeDtypeStruct((M, D), BF16),
        scratch_shapes=[pltpu.VMEM((3, W, tn), BF16)],
        compiler_params=_params("arbitrary", "arbitrary"),
        name="branch_merge",
    )(out_a, out_b, out_c, proj, proj, proj, w_branch)


def _rank_rows(v, n):
    rowid = lax.broadcasted_iota(jnp.int32, v.shape, 0)
    cnt = jnp.zeros(v.shape, jnp.int32)
    for m in range(n):
        vm = v[m:m + 1, :]
        cnt = cnt + jnp.where(vm > v, 1, jnp.where((vm == v) & (m < rowid), 1, 0))
    return cnt


def _route(logits_t, bias):
    E = logits_t.shape[0]
    gsz = E // N_GROUPS
    scores = jax.nn.sigmoid(logits_t)
    choice = scores + bias
    ridx = lax.broadcasted_iota(jnp.int32, (gsz, logits_t.shape[1]), 0).astype(F32)
    gid = lax.broadcasted_iota(jnp.int32, (N_GROUPS, logits_t.shape[1]), 0)
    gscore = jnp.zeros((N_GROUPS, logits_t.shape[1]), F32)
    for g in range(N_GROUPS):
        cg = choice[g * gsz:(g + 1) * gsz, :]
        m1 = jnp.max(cg, axis=0, keepdims=True)
        first = jnp.min(jnp.where(cg == m1, ridx, float(gsz)), axis=0, keepdims=True)
        m2 = jnp.max(jnp.where(ridx == first, -jnp.inf, cg), axis=0, keepdims=True)
        gscore = jnp.where(gid == g, m1 + m2, gscore)
    gsel = _rank_rows(gscore, N_GROUPS) < TOPK_GROUPS
    masked = jnp.concatenate(
        [jnp.where(gsel[g:g + 1, :], choice[g * gsz:(g + 1) * gsz, :], -jnp.inf) for g in range(N_GROUPS)],
        axis=0)
    sel = _rank_rows(masked, E) < TOP_K
    w = jnp.where(sel, scores, 0.0)
    return sel, w / jnp.sum(w, axis=0, keepdims=True) * ROUTED_SCALE


def _pack_pairs(v):
    half = v.shape[1] // 2
    lo = lax.bitcast_convert_type(v[:, :half], jnp.uint32) >> 16
    hi = lax.bitcast_convert_type(v[:, half:], jnp.uint32) & jnp.uint32(0xFFFF0000)
    return hi | lo


def _unpack_pairs(p):
    lo = lax.bitcast_convert_type(p << 16, F32)
    hi = lax.bitcast_convert_type(p & jnp.uint32(0xFFFF0000), F32)
    return lo, hi


def _norm_route_kernel(x_ref, g_ref, sc_ref, sh_ref, rw_ref, rb_ref,
                       hp_ref, e8_ref, p8_ref, w8_ref, cnt_ref, carry_ref):
    @pl.when((pl.program_id(0) == 0) & (pl.program_id(1) == 0))
    def _():
        carry_ref[...] = jnp.zeros_like(carry_ref)

    hf = _modnorm(x_ref[...], g_ref[...], sc_ref[...], sh_ref[...])
    h = hf.astype(BF16)
    hp_ref[...] = _pack_pairs(h.astype(F32))
    h_lo = (hf - h.astype(F32)).astype(BF16)
    rw = rw_ref[...]
    rw_hi = rw.astype(BF16)
    rw_lo = (rw - rw_hi.astype(F32)).astype(BF16)
    logits = _nt_dot(rw_hi, h) + (_nt_dot(rw_hi, h_lo) + _nt_dot(rw_lo, h))
    sel, gates = _route(logits, rb_ref[...])
    E, ts = gates.shape
    self32 = jnp.where(sel, 1.0, 0.0)
    selb = self32.astype(BF16)

    def indicator(cond):
        return jnp.where(cond, 1.0, 0.0).astype(BF16)

    r = lax.broadcasted_iota(jnp.int32, (ts, ts), 0)
    c = lax.broadcasted_iota(jnp.int32, (ts, ts), 1)
    before = jnp.dot(selb, indicator(r < c), preferred_element_type=F32)
    pos = carry_ref[...] + before
    carry_ref[...] += jnp.sum(self32, axis=1, keepdims=True)
    cnt_ref[...] = jnp.broadcast_to(carry_ref[...], cnt_ref.shape)

    er = lax.broadcasted_iota(jnp.int32, (E, E), 0)
    ec = lax.broadcasted_iota(jnp.int32, (E, E), 1)
    slot = jnp.dot(indicator(ec < er), selb, preferred_element_type=F32)
    eid = lax.broadcasted_iota(jnp.int32, (E, ts), 0).astype(F32)
    for k in range(TOP_K):
        mk = sel & (slot == float(k))
        e8_ref[k:k + 1, :] = jnp.sum(jnp.where(mk, eid, 0.0), axis=0, keepdims=True).astype(jnp.int32)
        p8_ref[k:k + 1, :] = jnp.sum(jnp.where(mk, pos, 0.0), axis=0, keepdims=True).astype(jnp.int32)
        w8_ref[k:k + 1, :] = jnp.sum(jnp.where(mk, gates, 0.0), axis=0, keepdims=True)


def norm_route(x, g, sc, sh, router_w, router_bias, *, ts=256):
    B, S, D = x.shape
    E = router_w.shape[1]
    ts = min(ts, S)
    nt = S // ts
    T = B * S
    row = pl.BlockSpec((None, ts, D), lambda b, i: (b, i, 0))
    per_b = pl.BlockSpec((None, 1, D), lambda b, i: (b, 0, 0))
    slots = pl.BlockSpec((TOP_K, ts), lambda b, i: (0, b * nt + i))
    return pl.pallas_call(
        _norm_route_kernel,
        grid=(B, nt),
        in_specs=[row, pl.BlockSpec((1, D), lambda b, i: (0, 0)), per_b, per_b,
                  pl.BlockSpec((E, D), lambda b, i: (0, 0)),
                  pl.BlockSpec((E, 1), lambda b, i: (0, 0))],
        out_specs=[pl.BlockSpec((ts, D // 2), lambda b, i: (b * nt + i, 0)), slots, slots, slots,
                   pl.BlockSpec((E, LANES), lambda b, i: (0, 0))],
        out_shape=[jax.ShapeDtypeStruct((T, D // 2), jnp.uint32),
                   jax.ShapeDtypeStruct((TOP_K, T), jnp.int32),
                   jax.ShapeDtypeStruct((TOP_K, T), jnp.int32),
                   jax.ShapeDtypeStruct((TOP_K, T), F32),
                   jax.ShapeDtypeStruct((E, LANES), F32)],
        scratch_shapes=[pltpu.VMEM((E, 1), F32)],
        compiler_params=_params("arbitrary", "arbitrary"),
        name="norm_route",
    )(x, g.reshape(1, D), sc.reshape(B, 1, D), sh.reshape(B, 1, D),
      router_w.T, router_bias.reshape(E, 1))


def _moe_plan(e8, p8, counts, *, tm):
    E = counts.shape[0]
    P = e8.shape[0] * e8.shape[1]
    ntiles = P // tm
    cnt = counts.astype(jnp.int32)
    off = jnp.cumsum(cnt) - cnt
    onehot = e8[None, :, :] == jnp.arange(E, dtype=jnp.int32)[:, None, None]
    dest = p8 + jnp.sum(jnp.where(onehot, off[:, None, None], 0), axis=0)
    bnd = jnp.sort(jnp.concatenate([jnp.arange(ntiles, dtype=jnp.int32) * tm, off]))
    nxt = jnp.concatenate([bnd[1:], jnp.full((1,), P, jnp.int32)])
    tile = jnp.minimum(bnd // tm, ntiles - 1)
    owner = jnp.sum((off[None, :] <= bnd[:, None]).astype(jnp.int32), axis=1) - 1
    expert = lax.cummax(jnp.where(nxt > bnd, owner, 0), axis=0)
    tile_changes = (tile[1:] != tile[:-1]).astype(jnp.int32)
    one = jnp.ones((1,), jnp.int32)
    new_expert = jnp.concatenate([one, (expert[1:] != expert[:-1]).astype(jnp.int32)])
    first = jnp.concatenate([one, tile_changes])
    last = jnp.concatenate([tile_changes, one])
    meta = jnp.stack([tile, expert, bnd, nxt, new_expert, first, last])
    return dest, meta


def _swiglu(lo, hi, w13_ref, w2_ref):
    half = lo.shape[1]
    f = w2_ref.shape[0]
    ab = (jnp.dot(lo, w13_ref[:half, :], preferred_element_type=F32)
          + jnp.dot(hi, w13_ref[half:, :], preferred_element_type=F32))
    a, b = ab[:, :f], ab[:, f:]
    act = (a * jax.nn.sigmoid(a) * b).astype(BF16)
    return jnp.dot(act, w2_ref[...], preferred_element_type=F32)


def _dispatch_kernel(dest_ref, h_ref, xs_ref, sem, *, tt):
    def body(t, carry):
        for k in range(TOP_K):
            d = dest_ref[0, t * TOP_K + k]
            pltpu.make_async_copy(h_ref.at[pl.ds(t, 1)], xs_ref.at[pl.ds(d, 1)], sem).start()
        return carry

    lax.fori_loop(0, tt, body, 0)
    for k in range(TOP_K):
        pltpu.make_async_copy(h_ref, xs_ref.at[pl.ds(0, tt)], sem).wait()


def moe_dispatch(hp, dest_tok, *, tt=256):
    T, Dh = hp.shape
    tt = min(tt, T)
    kern = functools.partial(_dispatch_kernel, tt=tt)
    return pl.pallas_call(
        kern,
        grid=(T // tt,),
        in_specs=[pl.BlockSpec((None, 1, tt * TOP_K), lambda i: (i, 0, 0), memory_space=pltpu.SMEM),
                  pl.BlockSpec((tt, Dh), lambda i: (i, 0))],
        out_specs=pl.BlockSpec(memory_space=pl.ANY),
        out_shape=jax.ShapeDtypeStruct((T * TOP_K, Dh), jnp.uint32),
        scratch_shapes=[pltpu.SemaphoreType.DMA],
        compiler_params=_params("arbitrary"),
        name="moe_dispatch",
    )(dest_tok.reshape(T // tt, 1, tt * TOP_K), hp)


def _moe_gemm_kernel(meta_ref, xs_ref, w1_ref, w3_ref, w2_ref, o_ref, w13_ref, w2b_ref, acc_ref, *, tm):
    k = pl.program_id(0)
    tile, start, end = meta_ref[0, k], meta_ref[2, k], meta_ref[3, k]
    f = w2_ref.shape[0]

    @pl.when(meta_ref[4, k] == 1)
    def _():
        _cast_rows(w1_ref, w13_ref.at[:, :f], 256)
        _cast_rows(w3_ref, w13_ref.at[:, f:], 256)
        _cast_rows(w2_ref, w2b_ref, 256)

    @pl.when(meta_ref[5, k] == 1)
    def _():
        acc_ref[...] = jnp.zeros_like(acc_ref)

    @pl.when(end > start)
    def _():
        lo, hi = _unpack_pairs(xs_ref[...])
        y = _swiglu(lo.astype(BF16), hi.astype(BF16), w13_ref, w2b_ref)
        rows = tile * tm + lax.broadcasted_iota(jnp.int32, (tm, 1), 0)
        acc_ref[...] += jnp.where((rows >= start) & (rows < end), y, 0.0)

    @pl.when(meta_ref[6, k] == 1)
    def _():
        o_ref[...] = _pack_pairs(acc_ref[...].astype(BF16).astype(F32))


def moe_gemm(xs, meta, w1, w3, w2, layer, *, tm):
    P, Dh = xs.shape
    _, E, D, F = w1.shape
    n_items = meta.shape[1]
    kern = functools.partial(_moe_gemm_kernel, tm=tm)
    grid_spec = pltpu.PrefetchScalarGridSpec(
        num_scalar_prefetch=1,
        grid=(n_items,),
        in_specs=[pl.BlockSpec((tm, Dh), lambda k, m: (m[0, k], 0)),
                  pl.BlockSpec((None, None, D, F), lambda k, m: (layer, m[1, k], 0, 0)),
                  pl.BlockSpec((None, None, D, F), lambda k, m: (layer, m[1, k], 0, 0)),
                  pl.BlockSpec((None, None, F, D), lambda k, m: (layer, m[1, k], 0, 0))],
        out_specs=pl.BlockSpec((tm, Dh), lambda k, m: (m[0, k], 0)),
        scratch_shapes=[pltpu.VMEM((D, 2 * F), BF16), pltpu.VMEM((F, D), BF16), pltpu.VMEM((tm, D), F32)],
    )
    return pl.pallas_call(
        kern,
        grid_spec=grid_spec,
        out_shape=jax.ShapeDtypeStruct((P, Dh), jnp.uint32),
        compiler_params=_params("arbitrary"),
        name="moe_gemm",
    )(meta, xs, w1, w3, w2)


def _combine_kernel(dest_ref, ys_ref, w_ref, hp_ref, x_ref, gt_ref, s13_ref, s2_ref, g_ref, sc_ref, sh_ref,
                    xo_ref, ho_ref, buf_ref, sem, *, tt):
    def body(t, carry):
        for k in range(TOP_K):
            d = dest_ref[0, t * TOP_K + k]
            pltpu.make_async_copy(ys_ref.at[pl.ds(d, 1)], buf_ref.at[k, pl.ds(t, 1)], sem).start()
        return carry

    lax.fori_loop(0, tt, body, 0)

    lo, hi = _unpack_pairs(hp_ref[...])
    y = _swiglu(lo.astype(BF16), hi.astype(BF16), s13_ref, s2_ref)

    for k in range(TOP_K):
        pltpu.make_async_copy(ys_ref.at[pl.ds(0, tt)], buf_ref.at[k], sem).wait()
    w = w_ref[...]
    acc_lo = acc_hi = None
    for k in range(TOP_K):
        lo, hi = _unpack_pairs(buf_ref[k])
        wk = w[:, k:k + 1]
        acc_lo = wk * lo if acc_lo is None else acc_lo + wk * lo
        acc_hi = wk * hi if acc_hi is None else acc_hi + wk * hi
    y = y + jnp.concatenate([acc_lo, acc_hi], axis=1)
    x = x_ref[...] + gt_ref[...] * y
    xo_ref[...] = x
    ho_ref[...] = _modnorm(x, g_ref[...], sc_ref[...], sh_ref[...]).astype(ho_ref.dtype)


def moe_combine(ys, dest_tok, w_tok, hp, x, gate, s13, s2, g, sc, sh, *, out_dtype, tt=128):
    B, S, D = x.shape
    T = B * S
    Dh = hp.shape[1]
    tt = min(tt, S)
    nt = S // tt
    row = pl.BlockSpec((None, tt, D), lambda b, i: (b, i, 0))
    per_b = pl.BlockSpec((None, 1, D), lambda b, i: (b, 0, 0))

    def whole(a):
        return pl.BlockSpec(a.shape, lambda b, i: (0,) * a.ndim)

    kern = functools.partial(_combine_kernel, tt=tt)
    return pl.pallas_call(
        kern,
        grid=(B, nt),
        in_specs=[pl.BlockSpec((None, 1, tt * TOP_K), lambda b, i: (b * nt + i, 0, 0), memory_space=pltpu.SMEM),
                  pl.BlockSpec(memory_space=pl.ANY),
                  pl.BlockSpec((tt, TOP_K), lambda b, i: (b * nt + i, 0)),
                  pl.BlockSpec((tt, Dh), lambda b, i: (b * nt + i, 0)),
                  row, per_b, whole(s13), whole(s2),
                  pl.BlockSpec((1, D), lambda b, i: (0, 0)), per_b, per_b],
        out_specs=[row, row],
        out_shape=[jax.ShapeDtypeStruct((B, S, D), F32), jax.ShapeDtypeStruct((B, S, D), out_dtype)],
        scratch_shapes=[pltpu.VMEM((TOP_K, tt, Dh), jnp.uint32), pltpu.SemaphoreType.DMA],
        compiler_params=_params("arbitrary", "arbitrary"),
        name="moe_combine",
    )(dest_tok.reshape(T // tt, 1, tt * TOP_K), ys, w_tok, hp, x, gate.reshape(B, 1, D), s13, s2,
      g.reshape(1, D), sc.reshape(B, 1, D), sh.reshape(B, 1, D))


def kernel(x, c, w_ada, ada_table, norm1_g, norm2_g, w_in, diff_lambda, diff_subln_g, conv_w,
           w_branch, w_o, router_w, router_bias, exp_w1, exp_w3, exp_w2,
           shared_w1, shared_w3, shared_w2, final_g):
    B, S, D = x.shape
    L = w_in.shape[0]
    W = w_branch.shape[2]
    heads = W // HEAD_W
    T = B * S
    assert w_in.shape[2] == 9 * W + 3 * D and S % C_BLOCK == 0
    moe_tm = min(MOE_ROW_TILE, T)

    mod = ada_mod(c, w_ada, ada_table)

    def mod_of(l, k):
        return mod[l, :B, k * D:(k + 1) * D]

    zeros_bd = jnp.zeros((B, D), F32)
    h = modnorm(x, norm1_g[0], mod_of(0, 1), mod_of(0, 0))
    for l in range(L):
        proj = matmul_wcast(h.reshape(T, D), w_in, l)
        lambda_init = 0.8 - 0.6 * math.exp(-0.3 * l)
        out_a = diff_attention(proj, diff_lambda[l], diff_subln_g[l], batch=B, seq=S, heads=heads,
                               col0=0, lambda_init=lambda_init)
        out_b = short_conv_gate(proj, conv_w[l], batch=B, seq=S, width=W, col0=3 * W)
        out_c = moba_attention(proj, batch=B, seq=S, heads=heads, col0=6 * heads)
        merged = branch_merge(out_a, out_b, out_c, proj, w_branch, l, gate_col0=9 * W)
        x1 = matmul_residual(merged, w_o, l, x.reshape(T, D), mod_of(l, 2), seq=S).reshape(B, S, D)

        hp, e8, p8, w8, counts = norm_route(x1, norm2_g[l], mod_of(l, 4), mod_of(l, 3),
                                            router_w[l], router_bias[l])
        dest, meta = _moe_plan(e8, p8, counts[:, 0], tm=moe_tm)
        dest_tok = dest.T.reshape(T * TOP_K)
        xs = moe_dispatch(hp, dest_tok)
        ys = moe_gemm(xs, meta, exp_w1, exp_w3, exp_w2, l, tm=moe_tm)
        s13 = jnp.concatenate([shared_w1[l], shared_w3[l]], axis=1).astype(BF16)
        s2 = shared_w2[l].astype(BF16)
        if l + 1 < L:
            norm_args = (norm1_g[l + 1], mod_of(l + 1, 1), mod_of(l + 1, 0))
        else:
            norm_args = (final_g, zeros_bd, zeros_bd)
        x, h = moe_combine(ys, dest_tok, w8.T, hp, x1, mod_of(l, 5), s13, s2, *norm_args,
                           out_dtype=BF16 if l + 1 < L else F32)
    return h
```

```python
import functools
import math

import jax
import jax.numpy as jnp
from jax import lax
from jax.experimental import pallas as pl
from jax.experimental.pallas import tpu as pltpu

F32 = jnp.float32
BF16 = jnp.bfloat16

A_DH = 64
HEAD_W = 128
C_BLOCK = 256
C_TOPK = 3
N_GROUPS = 8
TOPK_GROUPS = 4
TOP_K = 8
ROUTED_SCALE = 2.5
N_MOD = 6
B_CONV = 3
EPS = 1e-6
SUBLN_EPS = 1e-5
NEG_BIAS = -2.0 ** 100

VMEM_LIMIT_BYTES = 56 * 1024 * 1024
LANES = 128
SUBLANES = 8
MOE_ROW_TILE = 256


def _params(*sem):
    return pltpu.CompilerParams(dimension_semantics=sem, vmem_limit_bytes=VMEM_LIMIT_BYTES)


def _nt_dot(a, b):
    return lax.dot_general(a, b, (((1,), (1,)), ((), ())), preferred_element_type=F32)


def _cast_rows(src_ref, dst_ref, rows):
    n = src_ref.shape[0]
    rows = min(rows, n)

    def body(r, carry):
        sl = pl.ds(pl.multiple_of(r * rows, rows), rows)
        dst_ref[sl, :] = src_ref[sl, :].astype(dst_ref.dtype)
        return carry

    lax.fori_loop(0, n // rows, body, 0)


def _ada_kernel(c_ref, w_ref, t_ref, o_ref):
    c = c_ref[...]
    a = (c * jax.nn.sigmoid(c)).astype(BF16)
    acc = jnp.dot(a, w_ref[...].astype(BF16), preferred_element_type=F32)
    o_ref[...] = acc[None, :, :] + t_ref[...]


def ada_mod(c, w_ada, ada_table, *, tn=512):
    B, D = c.shape
    L = ada_table.shape[0]
    N = w_ada.shape[1]
    tn = min(tn, N)
    cp = jnp.zeros((SUBLANES, D), F32).at[:B].set(c)
    table = ada_table.reshape(L, 1, N)
    return pl.pallas_call(
        _ada_kernel,
        grid=(N // tn,),
        in_specs=[pl.BlockSpec((SUBLANES, D), lambda j: (0, 0)),
                  pl.BlockSpec((D, tn), lambda j: (0, j)),
                  pl.BlockSpec((L, 1, tn), lambda j: (0, 0, j))],
        out_specs=pl.BlockSpec((L, SUBLANES, tn), lambda j: (0, 0, j)),
        out_shape=jax.ShapeDtypeStruct((L, SUBLANES, N), F32),
        compiler_params=_params("arbitrary"),
        name="ada_mod",
    )(cp, w_ada, table)


def _modnorm(x, g, sc, sh):
    ms = jnp.mean(x * x, axis=-1, keepdims=True)
    y = x * lax.rsqrt(ms + EPS) * g
    return y * (1.0 + sc) + sh


def _modnorm_kernel(x_ref, g_ref, sc_ref, sh_ref, o_ref):
    o_ref[...] = _modnorm(x_ref[...], g_ref[...], sc_ref[...], sh_ref[...]).astype(o_ref.dtype)


def modnorm(x, g, sc, sh, *, ts=256):
    B, S, D = x.shape
    ts = min(ts, S)
    row = pl.BlockSpec((None, ts, D), lambda b, i: (b, i, 0))
    per_b = pl.BlockSpec((None, 1, D), lambda b, i: (b, 0, 0))
    return pl.pallas_call(
        _modnorm_kernel,
        grid=(B, S // ts),
        in_specs=[row, pl.BlockSpec((1, D), lambda b, i: (0, 0)), per_b, per_b],
        out_specs=row,
        out_shape=jax.ShapeDtypeStruct((B, S, D), BF16),
        compiler_params=_params("arbitrary", "arbitrary"),
        name="modnorm",
    )(x, g.reshape(1, D), sc.reshape(B, 1, D), sh.reshape(B, 1, D))


def _mm_kernel(x_ref, w_ref, o_ref, wbf_ref):
    @pl.when(pl.program_id(1) == 0)
    def _():
        _cast_rows(w_ref, wbf_ref, 256)

    o_ref[...] = jnp.dot(x_ref[...], wbf_ref[...], preferred_element_type=F32).astype(o_ref.dtype)


def matmul_wcast(x, w, layer, *, tm=512, tn=1024, out_dtype=BF16):
    M, K = x.shape
    N = w.shape[2]
    tm, tn = min(tm, M), min(tn, N)
    return pl.pallas_call(
        _mm_kernel,
        grid=(N // tn, M // tm),
        in_specs=[pl.BlockSpec((tm, K), lambda j, i: (i, 0)),
                  pl.BlockSpec((None, K, tn), lambda j, i: (layer, 0, j))],
        out_specs=pl.BlockSpec((tm, tn), lambda j, i: (i, j)),
        out_shape=jax.ShapeDtypeStruct((M, N), out_dtype),
        scratch_shapes=[pltpu.VMEM((K, tn), BF16)],
        compiler_params=_params("arbitrary", "arbitrary"),
        name="matmul_wcast",
    )(x, w)


def _mm_res_kernel(a_ref, w_ref, x_ref, g_ref, o_ref, wbf_ref):
    @pl.when(pl.program_id(1) == 0)
    def _():
        _cast_rows(w_ref, wbf_ref, 256)

    acc = jnp.dot(a_ref[...], wbf_ref[...], preferred_element_type=F32)
    o_ref[...] = x_ref[...] + g_ref[...] * acc


def matmul_residual(a, w, layer, x, gate, *, seq, tm=512, tn=512):
    M, K = a.shape
    N = w.shape[2]
    B = gate.shape[0]
    tm, tn = min(tm, seq), min(tn, N)
    tiles_per_b = seq // tm
    return pl.pallas_call(
        _mm_res_kernel,
        grid=(N // tn, M // tm),
        in_specs=[pl.BlockSpec((tm, K), lambda j, i: (i, 0)),
                  pl.BlockSpec((None, K, tn), lambda j, i: (layer, 0, j)),
                  pl.BlockSpec((tm, tn), lambda j, i: (i, j)),
                  pl.BlockSpec((None, 1, tn), lambda j, i: (i // tiles_per_b, 0, j))],
        out_specs=pl.BlockSpec((tm, tn), lambda j, i: (i, j)),
        out_shape=jax.ShapeDtypeStruct((M, N), F32),
        scratch_shapes=[pltpu.VMEM((K, tn), BF16)],
        compiler_params=_params("arbitrary", "arbitrary"),
        name="matmul_residual",
    )(a, w, x, gate.reshape(B, 1, N))


def _softmax_pv(s_past, s_diag, v1_ref, kend, exp_scale):
    m = jnp.max(s_diag, axis=-1, keepdims=True)
    if s_past is not None:
        m = jnp.maximum(m, jnp.max(s_past, axis=-1, keepdims=True))

    def prob(s):
        z = s - m
        return (jnp.exp(z) if exp_scale is None else jnp.exp2(z * exp_scale)).astype(BF16)

    p = prob(s_diag) if s_past is None else jnp.concatenate([prob(s_past), prob(s_diag)], axis=1)
    ol = jnp.dot(p, v1_ref[:kend, :], preferred_element_type=F32)
    return ol[:, :HEAD_W] / ol[:, HEAD_W:HEAD_W + 1]


def _diffattn_kernel(dl_ref, g_ref, q_ref, k_ref, v_ref, o_ref, v1_ref, *, seq, tq, lambda_init):
    dl = dl_ref[...]
    lam = (jnp.exp(jnp.sum(dl[0:1] * dl[1:2], keepdims=True))
           - jnp.exp(jnp.sum(dl[2:3] * dl[3:4], keepdims=True)) + lambda_init)
    v1_ref[:, :HEAD_W] = v_ref[...]
    v1_ref[:, HEAD_W:] = jnp.ones((seq, HEAD_W), BF16)

    lane = lax.broadcasted_iota(jnp.int32, (tq, HEAD_W), 1)
    r = lax.broadcasted_iota(jnp.int32, (2 * tq, tq), 0)
    r = jnp.where(r >= tq, r - tq, r)
    causal = lax.broadcasted_iota(jnp.int32, (2 * tq, tq), 1) <= r
    for i in range(seq // tq):
        kend = (i + 1) * tq
        q = q_ref[i * tq:kend, :] * jnp.asarray(A_DH ** -0.5, BF16)
        zero = jnp.zeros_like(q)
        qq = jnp.concatenate([jnp.where(lane < A_DH, q, zero), jnp.where(lane >= A_DH, q, zero)], axis=0)
        s = _nt_dot(qq, k_ref[:kend, :])
        s_diag = jnp.where(causal, s[:, kend - tq:], -jnp.inf)
        o = _softmax_pv(s[:, :kend - tq] if i else None, s_diag, v1_ref, kend, None)
        a = o[:tq] - lam * o[tq:]
        ms = jnp.mean(a * a, axis=-1, keepdims=True)
        y = a * lax.rsqrt(ms + SUBLN_EPS) * g_ref[...]
        o_ref[i * tq:kend, :] = (y * (1.0 - lambda_init)).astype(o_ref.dtype)


def diff_attention(proj, diff_lambda, subln_g, *, batch, seq, heads, col0, lambda_init, tq=256):
    tq = min(tq, seq)
    kern = functools.partial(_diffattn_kernel, seq=seq, tq=tq, lambda_init=lambda_init)

    def head(k):
        return pl.BlockSpec((seq, HEAD_W), lambda b, h: (b, col0 + k * heads + h))

    return pl.pallas_call(
        kern,
        grid=(batch, heads),
        in_specs=[pl.BlockSpec((4, A_DH), lambda b, h: (0, 0)),
                  pl.BlockSpec((1, HEAD_W), lambda b, h: (0, 0)),
                  head(0), head(1), head(2)],
        out_specs=pl.BlockSpec((seq, HEAD_W), lambda b, h: (b, h)),
        out_shape=jax.ShapeDtypeStruct((batch * seq, heads * HEAD_W), BF16),
        scratch_shapes=[pltpu.VMEM((seq, 2 * HEAD_W), BF16)],
        compiler_params=_params("arbitrary", "arbitrary"),
        name="diff_attention",
    )(diff_lambda, subln_g.reshape(1, HEAD_W), proj, proj, proj)


def _conv_kernel(b_ref, c_ref, h_ref, w_ref, o_ref):
    u = c_ref[...].astype(F32) * h_ref[...].astype(F32)
    row = lax.broadcasted_iota(jnp.int32, u.shape, 0)
    w = w_ref[...]
    acc = w[B_CONV - 1:B_CONV] * u
    for d in range(1, B_CONV):
        shifted = jnp.where(row >= d, pltpu.roll(u, d, axis=0), 0.0)
        acc = acc + w[B_CONV - 1 - d:B_CONV - d] * shifted
    o_ref[...] = (b_ref[...].astype(F32) * acc).astype(o_ref.dtype)


def short_conv_gate(proj, conv_w, *, batch, seq, width, col0, cw=256):
    cw = min(cw, width)
    nc = width // cw
    c0 = col0 // cw

    def col(k):
        return pl.BlockSpec((seq, cw), lambda b, j: (b, c0 + k * nc + j))

    return pl.pallas_call(
        _conv_kernel,
        grid=(batch, nc),
        in_specs=[col(0), col(1), col(2), pl.BlockSpec((B_CONV, cw), lambda b, j: (0, j))],
        out_specs=pl.BlockSpec((seq, cw), lambda b, j: (b, j)),
        out_shape=jax.ShapeDtypeStruct((batch * seq, width), BF16),
        compiler_params=_params("arbitrary", "arbitrary"),
        name="short_conv_gate",
    )(proj, proj, proj, conv_w)


def _moba_kernel(q_ref, k_ref, v_ref, o_ref, v1_ref, ka_ref, *, seq):
    blk = C_BLOCK
    nb = seq // blk
    nbp = 2 * SUBLANES
    assert nb <= nbp
    v1_ref[:, :HEAD_W] = v_ref[...]
    v1_ref[:, HEAD_W:] = jnp.ones((seq, HEAD_W), BF16)
    ka_ref[:, :HEAD_W] = k_ref[...]
    key_blk = lax.broadcasted_iota(jnp.int32, (seq, HEAD_W), 0) // blk
    ka_ref[:, HEAD_W:] = jnp.where(key_blk == lax.broadcasted_iota(jnp.int32, (seq, HEAD_W), 1),
                                   1.0, 0.0).astype(BF16)

    rowid = lax.broadcasted_iota(jnp.int32, (nbp, HEAD_W), 0)
    km = jnp.zeros((nbp, HEAD_W), F32)
    for n in range(nb):
        mean_n = jnp.mean(k_ref[n * blk:(n + 1) * blk, :].astype(F32), axis=0, keepdims=True)
        km = jnp.where(rowid == n, mean_n, km)
    km_hi = km.astype(BF16)
    km_lo = (km - km_hi.astype(F32)).astype(BF16)

    kpos = lax.broadcasted_iota(jnp.int32, (blk, blk), 1)
    causal = kpos <= lax.broadcasted_iota(jnp.int32, (blk, blk), 0)
    blkid = lax.broadcasted_iota(jnp.int32, (nbp, blk), 0)
    exp_scale = HEAD_W ** -0.5 * math.log2(math.e)
    for i in range(nb):
        kend = (i + 1) * blk
        q = q_ref[i * blk:kend, :]
        if i <= C_TOPK:
            s = _nt_dot(q, k_ref[:kend, :])
        else:
            gate = _nt_dot(km_hi, q) + _nt_dot(km_lo, q)
            cnt = jnp.zeros(gate.shape, jnp.int32)
            for m in range(i):
                gm = gate[m:m + 1, :]
                cnt = cnt + jnp.where(gm > gate, 1, jnp.where((gm == gate) & (m < blkid), 1, 0))
            keep = ((blkid < i) & (cnt < C_TOPK)) | (blkid == i)
            bias_t = jnp.where(keep, 0.0, NEG_BIAS)
            pad = jnp.zeros((HEAD_W - nbp, blk), F32)
            bias = jnp.concatenate([bias_t, pad], axis=0).T.astype(BF16)
            s = _nt_dot(jnp.concatenate([q, bias], axis=1), ka_ref[:kend, :])
        s_diag = jnp.where(causal, s[:, kend - blk:], -jnp.inf)
        o = _softmax_pv(s[:, :kend - blk] if i else None, s_diag, v1_ref, kend, exp_scale)
        o_ref[i * blk:kend, :] = o.astype(o_ref.dtype)


def moba_attention(proj, *, batch, seq, heads, col0):
    kern = functools.partial(_moba_kernel, seq=seq)

    def head(k):
        return pl.BlockSpec((seq, HEAD_W), lambda b, h: (b, col0 + k * heads + h))

    return pl.pallas_call(
        kern,
        grid=(batch, heads),
        in_specs=[head(0), head(1), head(2)],
        out_specs=pl.BlockSpec((seq, HEAD_W), lambda b, h: (b, h)),
        out_shape=jax.ShapeDtypeStruct((batch * seq, heads * HEAD_W), BF16),
        scratch_shapes=[pltpu.VMEM((seq, 2 * HEAD_W), BF16), pltpu.VMEM((seq, 2 * HEAD_W), BF16)],
        compiler_params=_params("arbitrary", "arbitrary"),
        name="moba_attention",
    )(proj, proj, proj)


def _branch_kernel(a_ref, b_ref, c_ref, ga_ref, gb_ref, gc_ref, w_ref, o_ref, wbf_ref):
    @pl.when(pl.program_id(1) == 0)
    def _():
        for g in range(3):
            _cast_rows(w_ref.at[g], wbf_ref.at[g], 256)

    acc = None
    for g, (x_ref, gl_ref) in enumerate(((a_ref, ga_ref), (b_ref, gb_ref), (c_ref, gc_ref))):
        br = jnp.dot(x_ref[...], wbf_ref[g], preferred_element_type=F32)
        term = jax.nn.sigmoid(gl_ref[...].astype(F32)) * br
        acc = term if acc is None else acc + term
    o_ref[...] = acc.astype(o_ref.dtype)


def branch_merge(out_a, out_b, out_c, proj, w_branch, layer, *, gate_col0, tm=512, tn=512):
    M, W = out_a.shape
    D = w_branch.shape[3]
    tm, tn = min(tm, M), min(tn, D)
    g0 = gate_col0 // tn
    gstride = D // tn
    x_spec = pl.BlockSpec((tm, W), lambda j, i: (i, 0))

    def gate_spec(g):
        return pl.BlockSpec((tm, tn), lambda j, i: (i, g0 + g * gstride + j))

    return pl.pallas_call(
        _branch_kernel,
        grid=(D // tn, M // tm),
        in_specs=[x_spec, x_spec, x_spec, gate_spec(0), gate_spec(1), gate_spec(2),
                  pl.BlockSpec((None, 3, W, tn), lambda j, i: (layer, 0, 0, j))],
        out_specs=pl.BlockSpec((tm, tn), lambda j, i: (i, j)),
        out_shape=jax.ShapeDtypeStruct((M, D), BF16),
        scratch_shapes=[pltpu.VMEM((3, W, tn), BF16)],
        compiler_params=_params("arbitrary", "arbitrary"),
        name="branch_merge",
    )(out_a, out_b, out_c, proj, proj, proj, w_branch)


def _rank_rows(v, n):
    rowid = lax.broadcasted_iota(jnp.int32, v.shape, 0)
    cnt = jnp.zeros(v.shape, jnp.int32)
    for m in range(n):
        vm = v[m:m + 1, :]
        cnt = cnt + jnp.where(vm > v, 1, jnp.where((vm == v) & (m < rowid), 1, 0))
    return cnt


def _route(logits_t, bias):
    E = logits_t.shape[0]
    gsz = E // N_GROUPS
    scores = jax.nn.sigmoid(logits_t)
    choice = scores + bias
    ridx = lax.broadcasted_iota(jnp.int32, (gsz, logits_t.shape[1]), 0).astype(F32)
    gid = lax.broadcasted_iota(jnp.int32, (N_GROUPS, logits_t.shape[1]), 0)
    gscore = jnp.zeros((N_GROUPS, logits_t.shape[1]), F32)
    for g in range(N_GROUPS):
        cg = choice[g * gsz:(g + 1) * gsz, :]
        m1 = jnp.max(cg, axis=0, keepdims=True)
        first = jnp.min(jnp.where(cg == m1, ridx, float(gsz)), axis=0, keepdims=True)
        m2 = jnp.max(jnp.where(ridx == first, -jnp.inf, cg), axis=0, keepdims=True)
        gscore = jnp.where(gid == g, m1 + m2, gscore)
    gsel = _rank_rows(gscore, N_GROUPS) < TOPK_GROUPS
    masked = jnp.concatenate(
        [jnp.where(gsel[g:g + 1, :], choice[g * gsz:(g + 1) * gsz, :], -jnp.inf) for g in range(N_GROUPS)],
        axis=0)
    sel = _rank_rows(masked, E) < TOP_K
    w = jnp.where(sel, scores, 0.0)
    return sel, w / jnp.sum(w, axis=0, keepdims=True) * ROUTED_SCALE


def _pack_pairs(v):
    half = v.shape[1] // 2
    lo = lax.bitcast_convert_type(v[:, :half], jnp.uint32) >> 16
    hi = lax.bitcast_convert_type(v[:, half:], jnp.uint32) & jnp.uint32(0xFFFF0000)
    return hi | lo


def _unpack_pairs(p):
    lo = lax.bitcast_convert_type(p << 16, F32)
    hi = lax.bitcast_convert_type(p & jnp.uint32(0xFFFF0000), F32)
    return lo, hi


def _norm_route_kernel(x_ref, g_ref, sc_ref, sh_ref, rw_ref, rb_ref,
                       hp_ref, e8_ref, p8_ref, w8_ref, cnt_ref, carry_ref):
    @pl.when((pl.program_id(0) == 0) & (pl.program_id(1) == 0))
    def _():
        carry_ref[...] = jnp.zeros_like(carry_ref)

    hf = _modnorm(x_ref[...], g_ref[...], sc_ref[...], sh_ref[...])
    h = hf.astype(BF16)
    hp_ref[...] = _pack_pairs(h.astype(F32))
    h_lo = (hf - h.astype(F32)).astype(BF16)
    rw = rw_ref[...]
    rw_hi = rw.astype(BF16)
    rw_lo = (rw - rw_hi.astype(F32)).astype(BF16)
    logits = _nt_dot(rw_hi, h) + (_nt_dot(rw_hi, h_lo) + _nt_dot(rw_lo, h))
    sel, gates = _route(logits, rb_ref[...])
    E, ts = gates.shape
    self32 = jnp.where(sel, 1.0, 0.0)
    selb = self32.astype(BF16)

    def indicator(cond):
        return jnp.where(cond, 1.0, 0.0).astype(BF16)

    r = lax.broadcasted_iota(jnp.int32, (ts, ts), 0)
    c = lax.broadcasted_iota(jnp.int32, (ts, ts), 1)
    before = jnp.dot(selb, indicator(r < c), preferred_element_type=F32)
    pos = carry_ref[...] + before
    carry_ref[...] += jnp.sum(self32, axis=1, keepdims=True)
    cnt_ref[...] = jnp.broadcast_to(carry_ref[...], cnt_ref.shape)

    er = lax.broadcasted_iota(jnp.int32, (E, E), 0)
    ec = lax.broadcasted_iota(jnp.int32, (E, E), 1)
    slot = jnp.dot(indicator(ec < er), selb, preferred_element_type=F32)
    eid = lax.broadcasted_iota(jnp.int32, (E, ts), 0).astype(F32)
    for k in range(TOP_K):
        mk = sel & (slot == float(k))
        e8_ref[k:k + 1, :] = jnp.sum(jnp.where(mk, eid, 0.0), axis=0, keepdims=True).astype(jnp.int32)
        p8_ref[k:k + 1, :] = jnp.sum(jnp.where(mk, pos, 0.0), axis=0, keepdims=True).astype(jnp.int32)
        w8_ref[k:k + 1, :] = jnp.sum(jnp.where(mk, gates, 0.0), axis=0, keepdims=True)


def norm_route(x, g, sc, sh, router_w, router_bias, *, ts=256):
    B, S, D = x.shape
    E = router_w.shape[1]
    ts = min(ts, S)
    nt = S // ts
    T = B * S
    row = pl.BlockSpec((None, ts, D), lambda b, i: (b, i, 0))
    per_b = pl.BlockSpec((None, 1, D), lambda b, i: (b, 0, 0))
    slots = pl.BlockSpec((TOP_K, ts), lambda b, i: (0, b * nt + i))
    return pl.pallas_call(
        _norm_route_kernel,
        grid=(B, nt),
        in_specs=[row, pl.BlockSpec((1, D), lambda b, i: (0, 0)), per_b, per_b,
                  pl.BlockSpec((E, D), lambda b, i: (0, 0)),
                  pl.BlockSpec((E, 1), lambda b, i: (0, 0))],
        out_specs=[pl.BlockSpec((ts, D // 2), lambda b, i: (b * nt + i, 0)), slots, slots, slots,
                   pl.BlockSpec((E, LANES), lambda b, i: (0, 0))],
        out_shape=[jax.ShapeDtypeStruct((T, D // 2), jnp.uint32),
                   jax.ShapeDtypeStruct((TOP_K, T), jnp.int32),
                   jax.ShapeDtypeStruct((TOP_K, T), jnp.int32),
                   jax.ShapeDtypeStruct((TOP_K, T), F32),
                   jax.ShapeDtypeStruct((E, LANES), F32)],
        scratch_shapes=[pltpu.VMEM((E, 1), F32)],
        compiler_params=_params("arbitrary", "arbitrary"),
        name="norm_route",
    )(x, g.reshape(1, D), sc.reshape(B, 1, D), sh.reshape(B, 1, D),
      router_w.T, router_bias.reshape(E, 1))


def _moe_plan(e8, p8, counts, *, tm):
    E = counts.shape[0]
    P = e8.shape[0] * e8.shape[1]
    ntiles = P // tm
    cnt = counts.astype(jnp.int32)
    off = jnp.cumsum(cnt) - cnt
    onehot = e8[None, :, :] == jnp.arange(E, dtype=jnp.int32)[:, None, None]
    dest = p8 + jnp.sum(jnp.where(onehot, off[:, None, None], 0), axis=0)
    bnd = jnp.sort(jnp.concatenate([jnp.arange(ntiles, dtype=jnp.int32) * tm, off]))
    nxt = jnp.concatenate([bnd[1:], jnp.full((1,), P, jnp.int32)])
    tile = jnp.minimum(bnd // tm, ntiles - 1)
    owner = jnp.sum((off[None, :] <= bnd[:, None]).astype(jnp.int32), axis=1) - 1
    expert = lax.cummax(jnp.where(nxt > bnd, owner, 0), axis=0)
    new_expert = jnp.concatenate([jnp.ones((1,), jnp.int32), (expert[1:] != expert[:-1]).astype(jnp.int32)])
    meta = jnp.stack([tile, expert, bnd, nxt, new_expert])
    return dest, meta


def _swiglu(lo, hi, w13_ref, w2_ref):
    half = lo.shape[1]
    f = w2_ref.shape[0]
    ab = (jnp.dot(lo, w13_ref[:half, :], preferred_element_type=F32)
          + jnp.dot(hi, w13_ref[half:, :], preferred_element_type=F32))
    a, b = ab[:, :f], ab[:, f:]
    act = (a * jax.nn.sigmoid(a) * b).astype(BF16)
    return jnp.dot(act, w2_ref[...], preferred_element_type=F32)


def _dispatch_kernel(dest_ref, h_ref, xs_ref, sem, *, tt):
    def body(t, carry):
        for k in range(TOP_K):
            d = dest_ref[0, t * TOP_K + k]
            pltpu.make_async_copy(h_ref.at[pl.ds(t, 1)], xs_ref.at[pl.ds(d, 1)], sem).start(priority=k % 2)
        return carry

    lax.fori_loop(0, tt, body, 0)
    for k in range(TOP_K):
        pltpu.make_async_copy(h_ref, xs_ref.at[pl.ds(0, tt)], sem).wait()


def moe_dispatch(hp, dest_tok, *, tt=256):
    T, Dh = hp.shape
    tt = min(tt, T)
    kern = functools.partial(_dispatch_kernel, tt=tt)
    return pl.pallas_call(
        kern,
        grid=(T // tt,),
        in_specs=[pl.BlockSpec((None, 1, tt * TOP_K), lambda i: (i, 0, 0), memory_space=pltpu.SMEM),
                  pl.BlockSpec((tt, Dh), lambda i: (i, 0))],
        out_specs=pl.BlockSpec(memory_space=pl.ANY),
        out_shape=jax.ShapeDtypeStruct((T * TOP_K, Dh), jnp.uint32),
        scratch_shapes=[pltpu.SemaphoreType.DMA],
        compiler_params=_params("arbitrary"),
        name="moe_dispatch",
    )(dest_tok.reshape(T // tt, 1, tt * TOP_K), hp)


def _moe_gemm_kernel(meta_ref, xs_ref, w1_ref, w3_ref, w2_ref, o_ref, w13_ref, w2b_ref, acc_ref, *, tm):
    k = pl.program_id(0)
    tile, start, end = meta_ref[0, k], meta_ref[2, k], meta_ref[3, k]
    f = w2_ref.shape[0]

    @pl.when(meta_ref[4, k] == 1)
    def _():
        _cast_rows(w1_ref, w13_ref.at[:, :f], 256)
        _cast_rows(w3_ref, w13_ref.at[:, f:], 256)
        _cast_rows(w2_ref, w2b_ref, 256)

    opens = start == tile * tm
    closes = end == (tile + 1) * tm

    def pack(y):
        return _pack_pairs(y.astype(BF16).astype(F32))

    @pl.when(end > start)
    def _():
        lo, hi = _unpack_pairs(xs_ref[...])
        y = _swiglu(lo.astype(BF16), hi.astype(BF16), w13_ref, w2b_ref)

        @pl.when(opens & closes)
        def _():
            o_ref[...] = pack(y)

        @pl.when(jnp.logical_not(opens & closes))
        def _():
            rows = tile * tm + lax.broadcasted_iota(jnp.int32, (tm, 1), 0)
            mine = jnp.where((rows >= start) & (rows < end), y, 0.0)

            @pl.when(opens)
            def _():
                acc_ref[...] = mine

            @pl.when(jnp.logical_not(opens | closes))
            def _():
                acc_ref[...] += mine

            @pl.when(closes)
            def _():
                o_ref[...] = pack(acc_ref[...] + mine)


def moe_gemm(xs, meta, w1, w3, w2, layer, *, tm):
    P, Dh = xs.shape
    _, E, D, F = w1.shape
    n_items = meta.shape[1]
    kern = functools.partial(_moe_gemm_kernel, tm=tm)
    grid_spec = pltpu.PrefetchScalarGridSpec(
        num_scalar_prefetch=1,
        grid=(n_items,),
        in_specs=[pl.BlockSpec((tm, Dh), lambda k, m: (m[0, k], 0)),
                  pl.BlockSpec((None, None, D, F), lambda k, m: (layer, m[1, k], 0, 0)),
                  pl.BlockSpec((None, None, D, F), lambda k, m: (layer, m[1, k], 0, 0)),
                  pl.BlockSpec((None, None, F, D), lambda k, m: (layer, m[1, k], 0, 0))],
        out_specs=pl.BlockSpec((tm, Dh), lambda k, m: (m[0, k], 0)),
        scratch_shapes=[pltpu.VMEM((D, 2 * F), BF16), pltpu.VMEM((F, D), BF16), pltpu.VMEM((tm, D), F32)],
    )
    return pl.pallas_call(
        kern,
        grid_spec=grid_spec,
        out_shape=jax.ShapeDtypeStruct((P, Dh), jnp.uint32),
        compiler_params=_params("arbitrary"),
        name="moe_gemm",
    )(meta, xs, w1, w3, w2)


def _combine_kernel(dest_ref, ys_ref, w_ref, hp_ref, x_ref, gt_ref, s13_ref, s2_ref, g_ref, sc_ref, sh_ref,
                    xo_ref, ho_ref, buf_ref, sem, *, tt):
    def body(t, carry):
        for k in range(TOP_K):
            d = dest_ref[0, t * TOP_K + k]
            pltpu.make_async_copy(ys_ref.at[pl.ds(d, 1)], buf_ref.at[k, pl.ds(t, 1)], sem).start(priority=k % 2)
        return carry

    lax.fori_loop(0, tt, body, 0)

    lo, hi = _unpack_pairs(hp_ref[...])
    y = _swiglu(lo.astype(BF16), hi.astype(BF16), s13_ref, s2_ref)

    for k in range(TOP_K):
        pltpu.make_async_copy(ys_ref.at[pl.ds(0, tt)], buf_ref.at[k], sem).wait()
    w = w_ref[...]
    acc_lo = acc_hi = None
    for k in range(TOP_K):
        lo, hi = _unpack_pairs(buf_ref[k])
        wk = w[:, k:k + 1]
        acc_lo = wk * lo if acc_lo is None else acc_lo + wk * lo
        acc_hi = wk * hi if acc_hi is None else acc_hi + wk * hi
    y = y + jnp.concatenate([acc_lo, acc_hi], axis=1)
    x = x_ref[...] + gt_ref[...] * y
    xo_ref[...] = x
    ho_ref[...] = _modnorm(x, g_ref[...], sc_ref[...], sh_ref[...]).astype(ho_ref.dtype)


def moe_combine(ys, dest_tok, w_tok, hp, x, gate, s13, s2, g, sc, sh, *, out_dtype, tt=128):
    B, S, D = x.shape
    T = B * S
    Dh = hp.shape[1]
    tt = min(tt, S)
    nt = S // tt
    row = pl.BlockSpec((None, tt, D), lambda b, i: (b, i, 0))
    per_b = pl.BlockSpec((None, 1, D), lambda b, i: (b, 0, 0))

    def whole(a):
        return pl.BlockSpec(a.shape, lambda b, i: (0,) * a.ndim)

    kern = functools.partial(_combine_kernel, tt=tt)
    return pl.pallas_call(
        kern,
        grid=(B, nt),
        in_specs=[pl.BlockSpec((None, 1, tt * TOP_K), lambda b, i: (b * nt + i, 0, 0), memory_space=pltpu.SMEM),
                  pl.BlockSpec(memory_space=pl.ANY),
                  pl.BlockSpec((tt, TOP_K), lambda b, i: (b * nt + i, 0)),
                  pl.BlockSpec((tt, Dh), lambda b, i: (b * nt + i, 0)),
                  row, per_b, whole(s13), whole(s2),
                  pl.BlockSpec((1, D), lambda b, i: (0, 0)), per_b, per_b],
        out_specs=[row, row],
        out_shape=[jax.ShapeDtypeStruct((B, S, D), F32), jax.ShapeDtypeStruct((B, S, D), out_dtype)],
        scratch_shapes=[pltpu.VMEM((TOP_K, tt, Dh), jnp.uint32), pltpu.SemaphoreType.DMA],
        compiler_params=_params("arbitrary", "arbitrary"),
        name="moe_combine",
    )(dest_tok.reshape(T // tt, 1, tt * TOP_K), ys, w_tok, hp, x, gate.reshape(B, 1, D), s13, s2,
      g.reshape(1, D), sc.reshape(B, 1, D), sh.reshape(B, 1, D))


def kernel(x, c, w_ada, ada_table, norm1_g, norm2_g, w_in, diff_lambda, diff_subln_g, conv_w,
           w_branch, w_o, router_w, router_bias, exp_w1, exp_w3, exp_w2,
           shared_w1, shared_w3, shared_w2, final_g):
    B, S, D = x.shape
    L = w_in.shape[0]
    W = w_branch.shape[2]
    heads = W // HEAD_W
    T = B * S
    assert w_in.shape[2] == 9 * W + 3 * D and S % C_BLOCK == 0
    moe_tm = min(MOE_ROW_TILE, T)

    mod = ada_mod(c, w_ada, ada_table)

    def mod_of(l, k):
        return mod[l, :B, k * D:(k + 1) * D]

    zeros_bd = jnp.zeros((B, D), F32)
    h = modnorm(x, norm1_g[0], mod_of(0, 1), mod_of(0, 0))
    for l in range(L):
        proj = matmul_wcast(h.reshape(T, D), w_in, l)
        lambda_init = 0.8 - 0.6 * math.exp(-0.3 * l)
        out_a = diff_attention(proj, diff_lambda[l], diff_subln_g[l], batch=B, seq=S, heads=heads,
                               col0=0, lambda_init=lambda_init)
        out_b = short_conv_gate(proj, conv_w[l], batch=B, seq=S, width=W, col0=3 * W)
        out_c = moba_attention(proj, batch=B, seq=S, heads=heads, col0=6 * heads)
        merged = branch_merge(out_a, out_b, out_c, proj, w_branch, l, gate_col0=9 * W)
        x1 = matmul_residual(merged, w_o, l, x.reshape(T, D), mod_of(l, 2), seq=S).reshape(B, S, D)

        hp, e8, p8, w8, counts = norm_route(x1, norm2_g[l], mod_of(l, 4), mod_of(l, 3),
                                            router_w[l], router_bias[l])
        dest, meta = _moe_plan(e8, p8, counts[:, 0], tm=moe_tm)
        dest_tok = dest.T.reshape(T * TOP_K)
        xs = moe_dispatch(hp, dest_tok)
        ys = moe_gemm(xs, meta, exp_w1, exp_w3, exp_w2, l, tm=moe_tm)
        s13 = jnp.concatenate([shared_w1[l], shared_w3[l]], axis=1).astype(BF16)
        s2 = shared_w2[l].astype(BF16)
        if l + 1 < L:
            norm_args = (norm1_g[l + 1], mod_of(l + 1, 1), mod_of(l + 1, 0))
        else:
            norm_args = (final_g, zeros_bd, zeros_bd)
        x, h = moe_combine(ys, dest_tok, w8.T, hp, x1, mod_of(l, 5), s13, s2, *norm_args,
                           out_dtype=BF16 if l + 1 < L else F32)
    return h
```

```python
import functools
import math

import jax
import jax.numpy as jnp
from jax import lax
from jax.experimental import pallas as pl
from jax.experimental.pallas import tpu as pltpu

F32 = jnp.float32
BF16 = jnp.bfloat16

A_DH = 64
HEAD_W = 128
C_BLOCK = 256
C_TOPK = 3
N_GROUPS = 8
TOPK_GROUPS = 4
TOP_K = 8
ROUTED_SCALE = 2.5
N_MOD = 6
B_CONV = 3
EPS = 1e-6
SUBLN_EPS = 1e-5
NEG_BIAS = -2.0 ** 100

VMEM_LIMIT_BYTES = 60 * 1024 * 1024
LANES = 128
SUBLANES = 8
MOE_ROW_TILE = 256
V1T_ROWS = HEAD_W + 2 * SUBLANES


def _params(*sem, **kw):
    return pltpu.CompilerParams(dimension_semantics=sem, vmem_limit_bytes=VMEM_LIMIT_BYTES, **kw)


def _nt_dot(a, b):
    return lax.dot_general(a, b, (((1,), (1,)), ((), ())), preferred_element_type=F32)


def _cast_rows(src_ref, dst_ref, rows):
    n = src_ref.shape[0]
    rows = min(rows, n)

    def body(r, carry):
        sl = pl.ds(pl.multiple_of(r * rows, rows), rows)
        dst_ref[sl, :] = src_ref[sl, :].astype(dst_ref.dtype)
        return carry

    lax.fori_loop(0, n // rows, body, 0)


def _ada_kernel(c_ref, w_ref, t_ref, o_ref):
    c = c_ref[...]
    a = (c * jax.nn.sigmoid(c)).astype(BF16)
    acc = jnp.dot(a, w_ref[...].astype(BF16), preferred_element_type=F32)
    o_ref[...] = acc[None, :, :] + t_ref[...]


def ada_mod(c, w_ada, ada_table, *, tn=512):
    B, D = c.shape
    L = ada_table.shape[0]
    N = w_ada.shape[1]
    tn = min(tn, N)
    cp = jnp.zeros((SUBLANES, D), F32).at[:B].set(c)
    table = ada_table.reshape(L, 1, N)
    return pl.pallas_call(
        _ada_kernel,
        grid=(N // tn,),
        in_specs=[pl.BlockSpec((SUBLANES, D), lambda j: (0, 0)),
                  pl.BlockSpec((D, tn), lambda j: (0, j)),
                  pl.BlockSpec((L, 1, tn), lambda j: (0, 0, j))],
        out_specs=pl.BlockSpec((L, SUBLANES, tn), lambda j: (0, 0, j)),
        out_shape=jax.ShapeDtypeStruct((L, SUBLANES, N), F32),
        compiler_params=_params("arbitrary"),
        name="ada_mod",
    )(cp, w_ada, table)


def _modnorm(x, g, sc, sh):
    ms = jnp.mean(x * x, axis=-1, keepdims=True)
    y = x * lax.rsqrt(ms + EPS) * g
    return y * (1.0 + sc) + sh


def _modnorm_kernel(x_ref, g_ref, sc_ref, sh_ref, o_ref):
    o_ref[...] = _modnorm(x_ref[...], g_ref[...], sc_ref[...], sh_ref[...]).astype(o_ref.dtype)


def modnorm(x, g, sc, sh, *, ts=256):
    B, S, D = x.shape
    ts = min(ts, S)
    row = pl.BlockSpec((None, ts, D), lambda b, i: (b, i, 0))
    per_b = pl.BlockSpec((None, 1, D), lambda b, i: (b, 0, 0))
    return pl.pallas_call(
        _modnorm_kernel,
        grid=(B, S // ts),
        in_specs=[row, pl.BlockSpec((1, D), lambda b, i: (0, 0)), per_b, per_b],
        out_specs=row,
        out_shape=jax.ShapeDtypeStruct((B, S, D), BF16),
        compiler_params=_params("arbitrary", "arbitrary"),
        name="modnorm",
    )(x, g.reshape(1, D), sc.reshape(B, 1, D), sh.reshape(B, 1, D))


def _mm_kernel(x_ref, w_ref, o_ref, wbf_ref):
    @pl.when(pl.program_id(1) == 0)
    def _():
        _cast_rows(w_ref, wbf_ref, 256)

    o_ref[...] = jnp.dot(x_ref[...], wbf_ref[...], preferred_element_type=F32).astype(o_ref.dtype)


def matmul_wcast(x, w, layer, *, tm=1024, tn=768, out_dtype=BF16):
    M, K = x.shape
    N = w.shape[2]
    tm, tn = min(tm, M), min(tn, N)
    return pl.pallas_call(
        _mm_kernel,
        grid=(N // tn, M // tm),
        in_specs=[pl.BlockSpec((tm, K), lambda j, i: (i, 0)),
                  pl.BlockSpec((None, K, tn), lambda j, i: (layer, 0, j))],
        out_specs=pl.BlockSpec((tm, tn), lambda j, i: (i, j)),
        out_shape=jax.ShapeDtypeStruct((M, N), out_dtype),
        scratch_shapes=[pltpu.VMEM((K, tn), BF16)],
        compiler_params=_params("arbitrary", "arbitrary"),
        name="matmul_wcast",
    )(x, w)


def _mm_res_kernel(a_ref, w_ref, x_ref, g_ref, o_ref, wbf_ref):
    @pl.when(pl.program_id(1) == 0)
    def _():
        _cast_rows(w_ref, wbf_ref, 256)

    acc = jnp.dot(a_ref[...], wbf_ref[...], preferred_element_type=F32)
    o_ref[...] = x_ref[...] + g_ref[...] * acc


def matmul_residual(a, w, layer, x, gate, *, seq, tm=512, tn=512):
    M, K = a.shape
    N = w.shape[2]
    B = gate.shape[0]
    tm, tn = min(tm, seq), min(tn, N)
    tiles_per_b = seq // tm
    return pl.pallas_call(
        _mm_res_kernel,
        grid=(N // tn, M // tm),
        in_specs=[pl.BlockSpec((tm, K), lambda j, i: (i, 0)),
                  pl.BlockSpec((None, K, tn), lambda j, i: (layer, 0, j)),
                  pl.BlockSpec((tm, tn), lambda j, i: (i, j)),
                  pl.BlockSpec((None, 1, tn), lambda j, i: (i // tiles_per_b, 0, j))],
        out_specs=pl.BlockSpec((tm, tn), lambda j, i: (i, j)),
        out_shape=jax.ShapeDtypeStruct((M, N), F32),
        scratch_shapes=[pltpu.VMEM((K, tn), BF16)],
        compiler_params=_params("arbitrary", "arbitrary"),
        name="matmul_residual",
    )(a, w, x, gate.reshape(B, 1, N))


def _softmax_pv(s_past, s_diag, v1_ref, kend, exp_scale):
    m = jnp.max(s_diag, axis=-1, keepdims=True)
    if s_past is not None:
        m = jnp.maximum(m, jnp.max(s_past, axis=-1, keepdims=True))

    def prob(s):
        z = s - m
        return (jnp.exp(z) if exp_scale is None else jnp.exp2(z * exp_scale)).astype(BF16)

    p = prob(s_diag) if s_past is None else jnp.concatenate([prob(s_past), prob(s_diag)], axis=1)
    ol = jnp.dot(p, v1_ref[:kend, :], preferred_element_type=F32)
    return ol[:, :HEAD_W] / ol[:, HEAD_W:HEAD_W + 1]


def _softmax_pv_t(st_past, st_diag, v1t_ref, kend, exp_scale):
    m = jnp.max(st_diag, axis=0, keepdims=True)
    if st_past is not None:
        m = jnp.maximum(m, jnp.max(st_past, axis=0, keepdims=True))

    def prob(s):
        z = s - m
        return (jnp.exp(z) if exp_scale is None else jnp.exp2(z * exp_scale)).astype(BF16)

    p = prob(st_diag) if st_past is None else jnp.concatenate([prob(st_past), prob(st_diag)], axis=0)
    ol = jnp.dot(v1t_ref[:, :kend], p, preferred_element_type=F32)
    return ol[:HEAD_W] / ol[HEAD_W:HEAD_W + 1]


def _fill_v1t(v_ref, v1t_ref, seq, blk):
    rid = lax.broadcasted_iota(jnp.int32, (V1T_ROWS - HEAD_W, seq), 0)
    for hh in range(v1t_ref.shape[0]):
        lanes = slice(hh * HEAD_W, (hh + 1) * HEAD_W)
        for n in range(seq // blk):
            cols = slice(n * blk, (n + 1) * blk)
            v1t_ref[hh, :HEAD_W, cols] = v_ref[cols, lanes].astype(F32).T.astype(BF16)
        v1t_ref[hh, HEAD_W:, :] = jnp.where(rid == 0, 1.0, 0.0).astype(BF16)


def _diffattn_kernel(dl_ref, g_ref, q_ref, k_ref, v_ref, o_ref, v1t_ref, *, seq, tq, lambda_init):
    hps = v1t_ref.shape[0]
    dl = dl_ref[...]
    lam = (jnp.exp(jnp.sum(dl[0:1] * dl[1:2], keepdims=True))
           - jnp.exp(jnp.sum(dl[2:3] * dl[3:4], keepdims=True)) + lambda_init)
    _fill_v1t(v_ref, v1t_ref, seq, tq)

    lane = lax.broadcasted_iota(jnp.int32, (tq, HEAD_W), 1)
    qpos = lax.broadcasted_iota(jnp.int32, (tq, 2 * tq), 1)
    qpos = jnp.where(qpos >= tq, qpos - tq, qpos)
    causal = lax.broadcasted_iota(jnp.int32, (tq, 2 * tq), 0) <= qpos
    for i in range(seq // tq):
        kend = (i + 1) * tq
        scores = []
        for hh in range(hps):
            lanes = slice(hh * HEAD_W, (hh + 1) * HEAD_W)
            q = q_ref[i * tq:kend, lanes] * jnp.asarray(A_DH ** -0.5, BF16)
            zero = jnp.zeros_like(q)
            qq = jnp.concatenate([jnp.where(lane < A_DH, q, zero), jnp.where(lane >= A_DH, q, zero)], axis=0)
            scores.append(_nt_dot(k_ref[:kend, lanes], qq))
        for hh in range(hps):
            lanes = slice(hh * HEAD_W, (hh + 1) * HEAD_W)
            st = scores[hh]
            st_diag = jnp.where(causal, st[kend - tq:, :], -jnp.inf)
            ot = _softmax_pv_t(st[:kend - tq, :] if i else None, st_diag, v1t_ref.at[hh], kend, None)
            at = ot[:, :tq] - lam * ot[:, tq:]
            ms = jnp.mean(at * at, axis=0, keepdims=True)
            yt = at * lax.rsqrt(ms + SUBLN_EPS) * g_ref[...]
            o_ref[i * tq:kend, lanes] = (yt * (1.0 - lambda_init)).T.astype(o_ref.dtype)


def diff_attention(proj, diff_lambda, subln_g, *, batch, seq, heads, col0, lambda_init, tq=256, hps=2):
    tq = min(tq, seq)
    hps = min(hps, heads)
    kern = functools.partial(_diffattn_kernel, seq=seq, tq=tq, lambda_init=lambda_init)
    hblocks = heads // hps

    def head(k):
        return pl.BlockSpec((seq, hps * HEAD_W), lambda b, h: (b, (col0 + k * heads) // hps + h))

    return pl.pallas_call(
        kern,
        grid=(batch, hblocks),
        in_specs=[pl.BlockSpec((4, A_DH), lambda b, h: (0, 0)),
                  pl.BlockSpec((HEAD_W, 1), lambda b, h: (0, 0)),
                  head(0), head(1), head(2)],
        out_specs=pl.BlockSpec((seq, hps * HEAD_W), lambda b, h: (b, h)),
        out_shape=jax.ShapeDtypeStruct((batch * seq, heads * HEAD_W), BF16),
        scratch_shapes=[pltpu.VMEM((hps, V1T_ROWS, seq), BF16)],
        compiler_params=_params("arbitrary", "arbitrary"),
        name="diff_attention",
    )(diff_lambda, subln_g.reshape(HEAD_W, 1), proj, proj, proj)


def _conv_kernel(b_ref, c_ref, h_ref, w_ref, o_ref):
    u = c_ref[...].astype(F32) * h_ref[...].astype(F32)
    row = lax.broadcasted_iota(jnp.int32, u.shape, 0)
    w = w_ref[...]
    acc = w[B_CONV - 1:B_CONV] * u
    for d in range(1, B_CONV):
        shifted = jnp.where(row >= d, pltpu.roll(u, d, axis=0), 0.0)
        acc = acc + w[B_CONV - 1 - d:B_CONV - d] * shifted
    o_ref[...] = (b_ref[...].astype(F32) * acc).astype(o_ref.dtype)


def short_conv_gate(proj, conv_w, *, batch, seq, width, col0, cw=256):
    cw = min(cw, width)
    nc = width // cw
    c0 = col0 // cw

    def col(k):
        return pl.BlockSpec((seq, cw), lambda b, j: (b, c0 + k * nc + j))

    return pl.pallas_call(
        _conv_kernel,
        grid=(batch, nc),
        in_specs=[col(0), col(1), col(2), pl.BlockSpec((B_CONV, cw), lambda b, j: (0, j))],
        out_specs=pl.BlockSpec((seq, cw), lambda b, j: (b, j)),
        out_shape=jax.ShapeDtypeStruct((batch * seq, width), BF16),
        compiler_params=_params("arbitrary", "arbitrary"),
        name="short_conv_gate",
    )(proj, proj, proj, conv_w)


def _moba_kernel(q_ref, k_ref, v_ref, o_ref, v1_ref, ka_ref, *, seq):
    blk = C_BLOCK
    nb = seq // blk
    nbp = 2 * SUBLANES
    assert nb <= nbp
    hps = v1_ref.shape[0]
    key_blk = lax.broadcasted_iota(jnp.int32, (seq, HEAD_W), 0) // blk
    indicator = jnp.where(key_blk == lax.broadcasted_iota(jnp.int32, (seq, HEAD_W), 1), 1.0, 0.0).astype(BF16)
    rowid = lax.broadcasted_iota(jnp.int32, (nbp, HEAD_W), 0)
    km_hi, km_lo = [], []
    for hh in range(hps):
        lanes = slice(hh * HEAD_W, (hh + 1) * HEAD_W)
        v1_ref[hh, :, :HEAD_W] = v_ref[:, lanes]
        v1_ref[hh, :, HEAD_W:] = jnp.ones((seq, HEAD_W), BF16)
        ka_ref[hh, :, :HEAD_W] = k_ref[:, lanes]
        ka_ref[hh, :, HEAD_W:] = indicator
        km = jnp.zeros((nbp, HEAD_W), F32)
        for n in range(nb):
            mean_n = jnp.mean(k_ref[n * blk:(n + 1) * blk, lanes].astype(F32), axis=0, keepdims=True)
            km = jnp.where(rowid == n, mean_n, km)
        km_hi.append(km.astype(BF16))
        km_lo.append((km - km_hi[hh].astype(F32)).astype(BF16))

    kpos = lax.broadcasted_iota(jnp.int32, (blk, blk), 1)
    causal = kpos <= lax.broadcasted_iota(jnp.int32, (blk, blk), 0)
    blkid = lax.broadcasted_iota(jnp.int32, (nbp, blk), 0)
    exp_scale = HEAD_W ** -0.5 * math.log2(math.e)
    for i in range(nb):
        kend = (i + 1) * blk
        scores = []
        for hh in range(hps):
            lanes = slice(hh * HEAD_W, (hh + 1) * HEAD_W)
            q = q_ref[i * blk:kend, lanes]
            if i <= C_TOPK:
                scores.append(_nt_dot(q, k_ref[:kend, lanes]))
                continue
            gate = _nt_dot(km_hi[hh], q) + _nt_dot(km_lo[hh], q)
            cnt = jnp.zeros(gate.shape, jnp.int32)
            for m in range(i):
                gm = gate[m:m + 1, :]
                cnt = cnt + jnp.where(gm > gate, 1, jnp.where((gm == gate) & (m < blkid), 1, 0))
            keep = ((blkid < i) & (cnt < C_TOPK)) | (blkid == i)
            bias_t = jnp.where(keep, 0.0, NEG_BIAS)
            pad = jnp.zeros((HEAD_W - nbp, blk), F32)
            bias = jnp.concatenate([bias_t, pad], axis=0).T.astype(BF16)
            scores.append(_nt_dot(jnp.concatenate([q, bias], axis=1), ka_ref[hh, :kend, :]))
        for hh in range(hps):
            s = scores[hh]
            s_diag = jnp.where(causal, s[:, kend - blk:], -jnp.inf)
            o = _softmax_pv(s[:, :kend - blk] if i else None, s_diag, v1_ref.at[hh], kend, exp_scale)
            o_ref[i * blk:kend, hh * HEAD_W:(hh + 1) * HEAD_W] = o.astype(o_ref.dtype)


def moba_attention(proj, *, batch, seq, heads, col0, hps=2):
    hps = min(hps, heads)
    kern = functools.partial(_moba_kernel, seq=seq)

    def head(k):
        return pl.BlockSpec((seq, hps * HEAD_W), lambda b, h: (b, (col0 + k * heads) // hps + h))

    return pl.pallas_call(
        kern,
        grid=(batch, heads // hps),
        in_specs=[head(0), head(1), head(2)],
        out_specs=pl.BlockSpec((seq, hps * HEAD_W), lambda b, h: (b, h)),
        out_shape=jax.ShapeDtypeStruct((batch * seq, heads * HEAD_W), BF16),
        scratch_shapes=[pltpu.VMEM((hps, seq, 2 * HEAD_W), BF16), pltpu.VMEM((hps, seq, 2 * HEAD_W), BF16)],
        compiler_params=_params("arbitrary", "arbitrary"),
        name="moba_attention",
    )(proj, proj, proj)


def _branch_kernel(a_ref, b_ref, c_ref, ga_ref, gb_ref, gc_ref, w_ref, o_ref, wbf_ref):
    @pl.when(pl.program_id(1) == 0)
    def _():
        for g in range(3):
            _cast_rows(w_ref.at[g], wbf_ref.at[g], 256)

    acc = None
    for g, (x_ref, gl_ref) in enumerate(((a_ref, ga_ref), (b_ref, gb_ref), (c_ref, gc_ref))):
        br = jnp.dot(x_ref[...], wbf_ref[g], preferred_element_type=F32)
        term = jax.nn.sigmoid(gl_ref[...].astype(F32)) * br
        acc = term if acc is None else acc + term
    o_ref[...] = acc.astype(o_ref.dtype)


def branch_merge(out_a, out_b, out_c, proj, w_branch, layer, *, gate_col0, tm=512, tn=512):
    M, W = out_a.shape
    D = w_branch.shape[3]
    tm, tn = min(tm, M), min(tn, D)
    g0 = gate_col0 // tn
    gstride = D // tn
    x_spec = pl.BlockSpec((tm, W), lambda j, i: (i, 0))

    def gate_spec(g):
        return pl.BlockSpec((tm, tn), lambda j, i: (i, g0 + g * gstride + j))

    return pl.pallas_call(
        _branch_kernel,
        grid=(D // tn, M // tm),
        in_specs=[x_spec, x_spec, x_spec, gate_spec(0), gate_spec(1), gate_spec(2),
                  pl.BlockSpec((None, 3, W, tn), lambda j, i: (layer, 0, 0, j))],
        out_specs=pl.BlockSpec((tm, tn), lambda j, i: (i, j)),
        out_shape=jax.ShapeDtypeStruct((M, D), BF16),
        scratch_shapes=[pltpu.VMEM((3, W, tn), BF16)],
        compiler_params=_params("arbitrary", "arbitrary"),
        name="branch_merge",
    )(out_a, out_b, out_c, proj, proj, proj, w_branch)


def _rank_rows(v, n):
    rowid = lax.broadcasted_iota(jnp.int32, v.shape, 0)
    cnt = jnp.zeros(v.shape, jnp.int32)
    for m in range(n):
        vm = v[m:m + 1, :]
        cnt = cnt + jnp.where(vm > v, 1, jnp.where((vm == v) & (m < rowid), 1, 0))
    return cnt


def _route(logits_t, bias):
    E = logits_t.shape[0]
    gsz = E // N_GROUPS
    scores = jax.nn.sigmoid(logits_t)
    choice = scores + bias
    ridx = lax.broadcasted_iota(jnp.int32, (gsz, logits_t.shape[1]), 0).astype(F32)
    gid = lax.broadcasted_iota(jnp.int32, (N_GROUPS, logits_t.shape[1]), 0)
    gscore = jnp.zeros((N_GROUPS, logits_t.shape[1]), F32)
    for g in range(N_GROUPS):
        cg = choice[g * gsz:(g + 1) * gsz, :]
        m1 = jnp.max(cg, axis=0, keepdims=True)
        first = jnp.min(jnp.where(cg == m1, ridx, float(gsz)), axis=0, keepdims=True)
        m2 = jnp.max(jnp.where(ridx == first, -jnp.inf, cg), axis=0, keepdims=True)
        gscore = jnp.where(gid == g, m1 + m2, gscore)
    gsel = _rank_rows(gscore, N_GROUPS) < TOPK_GROUPS
    masked = jnp.concatenate(
        [jnp.where(gsel[g:g + 1, :], choice[g * gsz:(g + 1) * gsz, :], -jnp.inf) for g in range(N_GROUPS)],
        axis=0)
    sel = _rank_rows(masked, E) < TOP_K
    w = jnp.where(sel, scores, 0.0)
    return sel, w / jnp.sum(w, axis=0, keepdims=True) * ROUTED_SCALE


def _pack_pairs(v):
    half = v.shape[1] // 2
    lo = lax.bitcast_convert_type(v[:, :half], jnp.uint32) >> 16
    hi = lax.bitcast_convert_type(v[:, half:], jnp.uint32) & jnp.uint32(0xFFFF0000)
    return hi | lo


def _unpack_pairs(p):
    lo = lax.bitcast_convert_type(p << 16, F32)
    hi = lax.bitcast_convert_type(p & jnp.uint32(0xFFFF0000), F32)
    return lo, hi


def _norm_route_kernel(x_ref, g_ref, sc_ref, sh_ref, rw_ref, rb_ref,
                       hp_ref, e8_ref, p8_ref, w8_ref, cnt_ref, carry_ref):
    @pl.when((pl.program_id(0) == 0) & (pl.program_id(1) == 0))
    def _():
        carry_ref[...] = jnp.zeros_like(carry_ref)

    hf = _modnorm(x_ref[...], g_ref[...], sc_ref[...], sh_ref[...])
    h = hf.astype(BF16)
    hp_ref[...] = _pack_pairs(h.astype(F32))
    h_lo = (hf - h.astype(F32)).astype(BF16)
    rw = rw_ref[...]
    rw_hi = rw.astype(BF16)
    rw_lo = (rw - rw_hi.astype(F32)).astype(BF16)
    logits = _nt_dot(rw_hi, h) + (_nt_dot(rw_hi, h_lo) + _nt_dot(rw_lo, h))
    sel, gates = _route(logits, rb_ref[...])
    E, ts = gates.shape
    self32 = jnp.where(sel, 1.0, 0.0)
    selb = self32.astype(BF16)

    def indicator(cond):
        return jnp.where(cond, 1.0, 0.0).astype(BF16)

    r = lax.broadcasted_iota(jnp.int32, (ts, ts), 0)
    c = lax.broadcasted_iota(jnp.int32, (ts, ts), 1)
    before = jnp.dot(selb, indicator(r < c), preferred_element_type=F32)
    pos = carry_ref[...] + before
    carry_ref[...] += jnp.sum(self32, axis=1, keepdims=True)
    cnt_ref[...] = jnp.broadcast_to(carry_ref[...], cnt_ref.shape)

    er = lax.broadcasted_iota(jnp.int32, (E, E), 0)
    ec = lax.broadcasted_iota(jnp.int32, (E, E), 1)
    slot = jnp.dot(indicator(ec < er), selb, preferred_element_type=F32)
    eid = lax.broadcasted_iota(jnp.int32, (E, ts), 0).astype(F32)
    for k in range(TOP_K):
        mk = sel & (slot == float(k))
        e8_ref[k:k + 1, :] = jnp.sum(jnp.where(mk, eid, 0.0), axis=0, keepdims=True).astype(jnp.int32)
        p8_ref[k:k + 1, :] = jnp.sum(jnp.where(mk, pos, 0.0), axis=0, keepdims=True).astype(jnp.int32)
        w8_ref[k:k + 1, :] = jnp.sum(jnp.where(mk, gates, 0.0), axis=0, keepdims=True)


def norm_route(x, g, sc, sh, router_w, router_bias, *, ts=256):
    B, S, D = x.shape
    E = router_w.shape[1]
    ts = min(ts, S)
    nt = S // ts
    T = B * S
    row = pl.BlockSpec((None, ts, D), lambda b, i: (b, i, 0))
    per_b = pl.BlockSpec((None, 1, D), lambda b, i: (b, 0, 0))
    slots = pl.BlockSpec((TOP_K, ts), lambda b, i: (0, b * nt + i))
    return pl.pallas_call(
        _norm_route_kernel,
        grid=(B, nt),
        in_specs=[row, pl.BlockSpec((1, D), lambda b, i: (0, 0)), per_b, per_b,
                  pl.BlockSpec((E, D), lambda b, i: (0, 0)),
                  pl.BlockSpec((E, 1), lambda b, i: (0, 0))],
        out_specs=[pl.BlockSpec((ts, D // 2), lambda b, i: (b * nt + i, 0)), slots, slots, slots,
                   pl.BlockSpec((E, LANES), lambda b, i: (0, 0))],
        out_shape=[jax.ShapeDtypeStruct((T, D // 2), jnp.uint32),
                   jax.ShapeDtypeStruct((TOP_K, T), jnp.int32),
                   jax.ShapeDtypeStruct((TOP_K, T), jnp.int32),
                   jax.ShapeDtypeStruct((TOP_K, T), F32),
                   jax.ShapeDtypeStruct((E, LANES), F32)],
        scratch_shapes=[pltpu.VMEM((E, 1), F32)],
        compiler_params=_params("arbitrary", "arbitrary"),
        name="norm_route",
    )(x, g.reshape(1, D), sc.reshape(B, 1, D), sh.reshape(B, 1, D),
      router_w.T, router_bias.reshape(E, 1))


def _moe_plan(e8, p8, counts, *, tm):
    E = counts.shape[0]
    P = e8.shape[0] * e8.shape[1]
    ntiles = P // tm
    cnt = counts.astype(jnp.int32)
    off = jnp.cumsum(cnt) - cnt
    onehot = e8[None, :, :] == jnp.arange(E, dtype=jnp.int32)[:, None, None]
    dest = p8 + jnp.sum(jnp.where(onehot, off[:, None, None], 0), axis=0)
    bnd = jnp.sort(jnp.concatenate([jnp.arange(ntiles, dtype=jnp.int32) * tm, off]))
    nxt = jnp.concatenate([bnd[1:], jnp.full((1,), P, jnp.int32)])
    tile = jnp.minimum(bnd // tm, ntiles - 1)
    owner = jnp.sum((off[None, :] <= bnd[:, None]).astype(jnp.int32), axis=1) - 1
    expert = lax.cummax(jnp.where(nxt > bnd, owner, 0), axis=0)
    new_expert = jnp.concatenate([jnp.ones((1,), jnp.int32), (expert[1:] != expert[:-1]).astype(jnp.int32)])
    meta = jnp.stack([tile, expert, bnd, nxt, new_expert])
    return dest, meta


def _swiglu(lo, hi, w13_ref, w2_ref):
    half = lo.shape[1]
    f = w2_ref.shape[0]
    ab = (jnp.dot(lo, w13_ref[:half, :], preferred_element_type=F32)
          + jnp.dot(hi, w13_ref[half:, :], preferred_element_type=F32))
    a, b = ab[:, :f], ab[:, f:]
    act = (a * jax.nn.sigmoid(a) * b).astype(BF16)
    return jnp.dot(act, w2_ref[...], preferred_element_type=F32)


def _dispatch_kernel(dest_ref, h_ref, xs_ref, sem, *, tt):
    def body(t, carry):
        for k in range(TOP_K):
            d = dest_ref[0, t * TOP_K + k]
            pltpu.make_async_copy(h_ref.at[pl.ds(t, 1)], xs_ref.at[pl.ds(d, 1)], sem).start(priority=k % 2)
        return carry

    lax.fori_loop(0, tt, body, 0)
    for k in range(TOP_K):
        pltpu.make_async_copy(h_ref, xs_ref.at[pl.ds(0, tt)], sem).wait()


def moe_dispatch(hp, dest_tok, *, tt=256):
    T, Dh = hp.shape
    tt = min(tt, T)
    kern = functools.partial(_dispatch_kernel, tt=tt)
    return pl.pallas_call(
        kern,
        grid=(T // tt,),
        in_specs=[pl.BlockSpec((None, 1, tt * TOP_K), lambda i: (i, 0, 0), memory_space=pltpu.SMEM),
                  pl.BlockSpec((tt, Dh), lambda i: (i, 0))],
        out_specs=pl.BlockSpec(memory_space=pl.ANY),
        out_shape=jax.ShapeDtypeStruct((T * TOP_K, Dh), jnp.uint32),
        scratch_shapes=[pltpu.SemaphoreType.DMA],
        compiler_params=_params("arbitrary"),
        name="moe_dispatch",
    )(dest_tok.reshape(T // tt, 1, tt * TOP_K), hp)


def _moe_gemm_kernel(meta_ref, xs_ref, w1_ref, w3_ref, w2_ref, o_ref, w13_ref, w2b_ref, acc_ref, *, tm):
    k = pl.program_id(0)
    tile, start, end = meta_ref[0, k], meta_ref[2, k], meta_ref[3, k]
    f = w2_ref.shape[0]

    @pl.when(meta_ref[4, k] == 1)
    def _():
        _cast_rows(w1_ref, w13_ref.at[:, :f], 256)
        _cast_rows(w3_ref, w13_ref.at[:, f:], 256)
        _cast_rows(w2_ref, w2b_ref, 256)

    opens = start == tile * tm
    closes = end == (tile + 1) * tm

    def pack(y):
        return _pack_pairs(y.astype(BF16).astype(F32))

    @pl.when(end > start)
    def _():
        lo, hi = _unpack_pairs(xs_ref[...])
        y = _swiglu(lo.astype(BF16), hi.astype(BF16), w13_ref, w2b_ref)

        @pl.when(opens & closes)
        def _():
            o_ref[...] = pack(y)

        @pl.when(jnp.logical_not(opens & closes))
        def _():
            rows = tile * tm + lax.broadcasted_iota(jnp.int32, (tm, 1), 0)
            mine = jnp.where((rows >= start) & (rows < end), y, 0.0)

            @pl.when(opens)
            def _():
                acc_ref[...] = mine

            @pl.when(jnp.logical_not(opens | closes))
            def _():
                acc_ref[...] += mine

            @pl.when(closes)
            def _():
                o_ref[...] = pack(acc_ref[...] + mine)


def moe_gemm(xs, meta, w1, w3, w2, layer, *, tm):
    P, Dh = xs.shape
    _, E, D, F = w1.shape
    n_items = meta.shape[1]
    kern = functools.partial(_moe_gemm_kernel, tm=tm)
    grid_spec = pltpu.PrefetchScalarGridSpec(
        num_scalar_prefetch=1,
        grid=(n_items,),
        in_specs=[pl.BlockSpec((tm, Dh), lambda k, m: (m[0, k], 0)),
                  pl.BlockSpec((None, None, D, F), lambda k, m: (layer, m[1, k], 0, 0)),
                  pl.BlockSpec((None, None, D, F), lambda k, m: (layer, m[1, k], 0, 0)),
                  pl.BlockSpec((None, None, F, D), lambda k, m: (layer, m[1, k], 0, 0))],
        out_specs=pl.BlockSpec((tm, Dh), lambda k, m: (m[0, k], 0)),
        scratch_shapes=[pltpu.VMEM((D, 2 * F), BF16), pltpu.VMEM((F, D), BF16), pltpu.VMEM((tm, D), F32)],
    )
    return pl.pallas_call(
        kern,
        grid_spec=grid_spec,
        out_shape=jax.ShapeDtypeStruct((P, Dh), jnp.uint32),
        compiler_params=_params("arbitrary"),
        name="moe_gemm",
    )(meta, xs, w1, w3, w2)


def _combine_kernel(dest_ref, ys_ref, w_ref, hp_ref, x_ref, gt_ref, s13_ref, s2_ref,
                    g_ref, sc_ref, sh_ref, xo_ref, ho_ref, buf_ref, sem, *, tt):
    def gather(t, carry):
        for k in range(TOP_K):
            d = dest_ref[0, t * TOP_K + k]
            pltpu.make_async_copy(ys_ref.at[pl.ds(d, 1)], buf_ref.at[k, pl.ds(t, 1)], sem).start(priority=k % 2)
        return carry

    lax.fori_loop(0, tt, gather, 0)

    lo, hi = _unpack_pairs(hp_ref[...])
    y = _swiglu(lo.astype(BF16), hi.astype(BF16), s13_ref, s2_ref)
    for k in range(TOP_K):
        pltpu.make_async_copy(ys_ref.at[pl.ds(0, tt)], buf_ref.at[k], sem).wait()
    w = w_ref[...]
    acc_lo = acc_hi = None
    for k in range(TOP_K):
        lo, hi = _unpack_pairs(buf_ref[k])
        wk = w[:, k:k + 1]
        acc_lo = wk * lo if acc_lo is None else acc_lo + wk * lo
        acc_hi = wk * hi if acc_hi is None else acc_hi + wk * hi
    y = y + jnp.concatenate([acc_lo, acc_hi], axis=1)
    x = x_ref[...] + gt_ref[...] * y
    xo_ref[...] = x
    ho_ref[...] = _modnorm(x, g_ref[...], sc_ref[...], sh_ref[...]).astype(ho_ref.dtype)


def moe_combine(ys, dest_tok, w_tok, hp, x, gate, s13, s2, g, sc, sh, *, out_dtype, tt=128):
    B, S, D = x.shape
    T = B * S
    Dh = hp.shape[1]
    tt = min(tt, S)
    nt = S // tt
    n_steps = T // tt
    row = pl.BlockSpec((tt, D), lambda i: (i, 0))
    per_b = pl.BlockSpec((None, 1, D), lambda i: (i // nt, 0, 0))

    def whole(a):
        return pl.BlockSpec(a.shape, lambda i: (0,) * a.ndim)

    kern = functools.partial(_combine_kernel, tt=tt)
    xo, ho = pl.pallas_call(
        kern,
        grid=(n_steps,),
        in_specs=[pl.BlockSpec((None, 1, tt * TOP_K), lambda i: (i, 0, 0), memory_space=pltpu.SMEM),
                  pl.BlockSpec(memory_space=pl.ANY),
                  pl.BlockSpec((tt, TOP_K), lambda i: (i, 0)),
                  pl.BlockSpec((tt, Dh), lambda i: (i, 0)),
                  row, per_b, whole(s13), whole(s2),
                  pl.BlockSpec((1, D), lambda i: (0, 0)), per_b, per_b],
        out_specs=[row, row],
        out_shape=[jax.ShapeDtypeStruct((T, D), F32), jax.ShapeDtypeStruct((T, D), out_dtype)],
        scratch_shapes=[pltpu.VMEM((TOP_K, tt, Dh), jnp.uint32), pltpu.SemaphoreType.DMA],
        compiler_params=_params("arbitrary"),
        name="moe_combine",
    )(dest_tok.reshape(n_steps, 1, tt * TOP_K), ys, w_tok, hp, x.reshape(T, D), gate.reshape(B, 1, D), s13, s2,
      g.reshape(1, D), sc.reshape(B, 1, D), sh.reshape(B, 1, D))
    return xo.reshape(B, S, D), ho.reshape(B, S, D)


def kernel(x, c, w_ada, ada_table, norm1_g, norm2_g, w_in, diff_lambda, diff_subln_g, conv_w,
           w_branch, w_o, router_w, router_bias, exp_w1, exp_w3, exp_w2,
           shared_w1, shared_w3, shared_w2, final_g):
    B, S, D = x.shape
    L = w_in.shape[0]
    W = w_branch.shape[2]
    heads = W // HEAD_W
    T = B * S
    assert w_in.shape[2] == 9 * W + 3 * D and S % C_BLOCK == 0
    moe_tm = min(MOE_ROW_TILE, T)

    mod = ada_mod(c, w_ada, ada_table)

    def mod_of(l, k):
        return mod[l, :B, k * D:(k + 1) * D]

    zeros_bd = jnp.zeros((B, D), F32)
    h = modnorm(x, norm1_g[0], mod_of(0, 1), mod_of(0, 0))
    for l in range(L):
        proj = matmul_wcast(h.reshape(T, D), w_in, l)
        lambda_init = 0.8 - 0.6 * math.exp(-0.3 * l)
        out_a = diff_attention(proj, diff_lambda[l], diff_subln_g[l], batch=B, seq=S, heads=heads,
                               col0=0, lambda_init=lambda_init)
        out_b = short_conv_gate(proj, conv_w[l], batch=B, seq=S, width=W, col0=3 * W)
        out_c = moba_attention(proj, batch=B, seq=S, heads=heads, col0=6 * heads)
        merged = branch_merge(out_a, out_b, out_c, proj, w_branch, l, gate_col0=9 * W)
        x1 = matmul_residual(merged, w_o, l, x.reshape(T, D), mod_of(l, 2), seq=S).reshape(B, S, D)

        hp, e8, p8, w8, counts = norm_route(x1, norm2_g[l], mod_of(l, 4), mod_of(l, 3),
                                            router_w[l], router_bias[l])
        dest, meta = _moe_plan(e8, p8, counts[:, 0], tm=moe_tm)
        dest_tok = dest.T.reshape(T * TOP_K)
        xs = moe_dispatch(hp, dest_tok)
        ys = moe_gemm(xs, meta, exp_w1, exp_w3, exp_w2, l, tm=moe_tm)
        s13 = jnp.concatenate([shared_w1[l], shared_w3[l]], axis=1).astype(BF16)
        s2 = shared_w2[l].astype(BF16)
        if l + 1 < L:
            norm_args = (norm1_g[l + 1], mod_of(l + 1, 1), mod_of(l + 1, 0))
        else:
            norm_args = (final_g, zeros_bd, zeros_bd)
        x, h = moe_combine(ys, dest_tok, w8.T, hp, x1, mod_of(l, 5), s13, s2, *norm_args,
                           out_dtype=BF16 if l + 1 < L else F32)
    return h
```

```python
import functools
import math

import jax
import jax.numpy as jnp
from jax import lax
from jax.experimental import pallas as pl
from jax.experimental.pallas import tpu as pltpu

F32 = jnp.float32
BF16 = jnp.bfloat16

A_DH = 64
HEAD_W = 128
C_BLOCK = 256
C_TOPK = 3
N_GROUPS = 8
TOPK_GROUPS = 4
TOP_K = 8
ROUTED_SCALE = 2.5
N_MOD = 6
B_CONV = 3
EPS = 1e-6
SUBLN_EPS = 1e-5
NEG_BIAS = -2.0 ** 100

VMEM_LIMIT_BYTES = 60 * 1024 * 1024
LANES = 128
SUBLANES = 8
MOE_ROW_TILE = 256
V1T_ROWS = HEAD_W + 2 * SUBLANES


def _params(*sem, **kw):
    return pltpu.CompilerParams(dimension_semantics=sem, vmem_limit_bytes=VMEM_LIMIT_BYTES, **kw)


def _nt_dot(a, b):
    return lax.dot_general(a, b, (((1,), (1,)), ((), ())), preferred_element_type=F32)


def _cast_rows(src_ref, dst_ref, rows):
    n = src_ref.shape[0]
    rows = min(rows, n)

    def body(r, carry):
        sl = pl.ds(pl.multiple_of(r * rows, rows), rows)
        dst_ref[sl, :] = src_ref[sl, :].astype(dst_ref.dtype)
        return carry

    lax.fori_loop(0, n // rows, body, 0)


def _ada_kernel(c_ref, w_ref, t_ref, o_ref):
    c = c_ref[...]
    a = (c * jax.nn.sigmoid(c)).astype(BF16)
    acc = jnp.dot(a, w_ref[...].astype(BF16), preferred_element_type=F32)
    o_ref[...] = acc[None, :, :] + t_ref[...]


def ada_mod(c, w_ada, ada_table, *, tn=512):
    B, D = c.shape
    L = ada_table.shape[0]
    N = w_ada.shape[1]
    tn = min(tn, N)
    cp = jnp.zeros((SUBLANES, D), F32).at[:B].set(c)
    table = ada_table.reshape(L, 1, N)
    return pl.pallas_call(
        _ada_kernel,
        grid=(N // tn,),
        in_specs=[pl.BlockSpec((SUBLANES, D), lambda j: (0, 0)),
                  pl.BlockSpec((D, tn), lambda j: (0, j)),
                  pl.BlockSpec((L, 1, tn), lambda j: (0, 0, j))],
        out_specs=pl.BlockSpec((L, SUBLANES, tn), lambda j: (0, 0, j)),
        out_shape=jax.ShapeDtypeStruct((L, SUBLANES, N), F32),
        compiler_params=_params("arbitrary"),
        name="ada_mod",
    )(cp, w_ada, table)


def _modnorm(x, g, sc, sh):
    ms = jnp.mean(x * x, axis=-1, keepdims=True)
    y = x * lax.rsqrt(ms + EPS) * g
    return y * (1.0 + sc) + sh


def _modnorm_kernel(x_ref, g_ref, sc_ref, sh_ref, o_ref):
    o_ref[...] = _modnorm(x_ref[...], g_ref[...], sc_ref[...], sh_ref[...]).astype(o_ref.dtype)


def modnorm(x, g, sc, sh, *, ts=256):
    B, S, D = x.shape
    ts = min(ts, S)
    row = pl.BlockSpec((None, ts, D), lambda b, i: (b, i, 0))
    per_b = pl.BlockSpec((None, 1, D), lambda b, i: (b, 0, 0))
    return pl.pallas_call(
        _modnorm_kernel,
        grid=(B, S // ts),
        in_specs=[row, pl.BlockSpec((1, D), lambda b, i: (0, 0)), per_b, per_b],
        out_specs=row,
        out_shape=jax.ShapeDtypeStruct((B, S, D), BF16),
        compiler_params=_params("arbitrary", "arbitrary"),
        name="modnorm",
    )(x, g.reshape(1, D), sc.reshape(B, 1, D), sh.reshape(B, 1, D))


def _mm_kernel(x_ref, w_ref, o_ref, wbf_ref):
    @pl.when(pl.program_id(1) == 0)
    def _():
        _cast_rows(w_ref, wbf_ref, 256)

    o_ref[...] = jnp.dot(x_ref[...], wbf_ref[...], preferred_element_type=F32).astype(o_ref.dtype)


def matmul_wcast(x, w, layer, *, tm=1024, tn=768, out_dtype=BF16):
    M, K = x.shape
    N = w.shape[2]
    tm, tn = min(tm, M), min(tn, N)
    return pl.pallas_call(
        _mm_kernel,
        grid=(N // tn, M // tm),
        in_specs=[pl.BlockSpec((tm, K), lambda j, i: (i, 0)),
                  pl.BlockSpec((None, K, tn), lambda j, i: (layer, 0, j))],
        out_specs=pl.BlockSpec((tm, tn), lambda j, i: (i, j)),
        out_shape=jax.ShapeDtypeStruct((M, N), out_dtype),
        scratch_shapes=[pltpu.VMEM((K, tn), BF16)],
        compiler_params=_params("arbitrary", "arbitrary"),
        name="matmul_wcast",
    )(x, w)


def _mm_res_kernel(a_ref, w_ref, x_ref, g_ref, o_ref, wbf_ref):
    @pl.when(pl.program_id(1) == 0)
    def _():
        _cast_rows(w_ref, wbf_ref, 256)

    acc = jnp.dot(a_ref[...], wbf_ref[...], preferred_element_type=F32)
    o_ref[...] = x_ref[...] + g_ref[...] * acc


def matmul_residual(a, w, layer, x, gate, *, seq, tm=512, tn=512):
    M, K = a.shape
    N = w.shape[2]
    B = gate.shape[0]
    tm, tn = min(tm, seq), min(tn, N)
    tiles_per_b = seq // tm
    return pl.pallas_call(
        _mm_res_kernel,
        grid=(N // tn, M // tm),
        in_specs=[pl.BlockSpec((tm, K), lambda j, i: (i, 0)),
                  pl.BlockSpec((None, K, tn), lambda j, i: (layer, 0, j)),
                  pl.BlockSpec((tm, tn), lambda j, i: (i, j)),
                  pl.BlockSpec((None, 1, tn), lambda j, i: (i // tiles_per_b, 0, j))],
        out_specs=pl.BlockSpec((tm, tn), lambda j, i: (i, j)),
        out_shape=jax.ShapeDtypeStruct((M, N), F32),
        scratch_shapes=[pltpu.VMEM((K, tn), BF16)],
        compiler_params=_params("arbitrary", "arbitrary"),
        name="matmul_residual",
    )(a, w, x, gate.reshape(B, 1, N))


def _softmax_pv(s_past, s_diag, v1_ref, kend, exp_scale):
    m = jnp.max(s_diag, axis=-1, keepdims=True)
    if s_past is not None:
        m = jnp.maximum(m, jnp.max(s_past, axis=-1, keepdims=True))

    def prob(s):
        z = s - m
        return (jnp.exp(z) if exp_scale is None else jnp.exp2(z * exp_scale)).astype(BF16)

    p = prob(s_diag) if s_past is None else jnp.concatenate([prob(s_past), prob(s_diag)], axis=1)
    ol = jnp.dot(p, v1_ref[:kend, :], preferred_element_type=F32)
    return ol[:, :HEAD_W] / ol[:, HEAD_W:HEAD_W + 1]


def _softmax_pv_t(st_past, st_diag, v1t_ref, kend, exp_scale):
    m = jnp.max(st_diag, axis=0, keepdims=True)
    if st_past is not None:
        m = jnp.maximum(m, jnp.max(st_past, axis=0, keepdims=True))

    def prob(s):
        z = s - m
        return (jnp.exp(z) if exp_scale is None else jnp.exp2(z * exp_scale)).astype(BF16)

    p = prob(st_diag) if st_past is None else jnp.concatenate([prob(st_past), prob(st_diag)], axis=0)
    ol = jnp.dot(v1t_ref[:, :kend], p, preferred_element_type=F32)
    return ol[:HEAD_W] / ol[HEAD_W:HEAD_W + 1]


def _fill_v1t(v_ref, v1t_ref, seq, blk):
    rid = lax.broadcasted_iota(jnp.int32, (V1T_ROWS - HEAD_W, seq), 0)
    for hh in range(v1t_ref.shape[0]):
        lanes = slice(hh * HEAD_W, (hh + 1) * HEAD_W)
        for n in range(seq // blk):
            cols = slice(n * blk, (n + 1) * blk)
            v1t_ref[hh, :HEAD_W, cols] = v_ref[cols, lanes].astype(F32).T.astype(BF16)
        v1t_ref[hh, HEAD_W:, :] = jnp.where(rid == 0, 1.0, 0.0).astype(BF16)


def _diffattn_kernel(dl_ref, g_ref, q_ref, k_ref, v_ref, o_ref, v1t_ref, *, seq, tq, lambda_init):
    hps = v1t_ref.shape[0]
    dl = dl_ref[...]
    lam = (jnp.exp(jnp.sum(dl[0:1] * dl[1:2], keepdims=True))
           - jnp.exp(jnp.sum(dl[2:3] * dl[3:4], keepdims=True)) + lambda_init)
    _fill_v1t(v_ref, v1t_ref, seq, tq)

    lane = lax.broadcasted_iota(jnp.int32, (tq, HEAD_W), 1)
    qpos = lax.broadcasted_iota(jnp.int32, (tq, 2 * tq), 1)
    qpos = jnp.where(qpos >= tq, qpos - tq, qpos)
    causal = lax.broadcasted_iota(jnp.int32, (tq, 2 * tq), 0) <= qpos

    def tile_scores(i):
        kend = (i + 1) * tq
        out = []
        for hh in range(hps):
            lanes = slice(hh * HEAD_W, (hh + 1) * HEAD_W)
            q = q_ref[i * tq:kend, lanes] * jnp.asarray(A_DH ** -0.5, BF16)
            zero = jnp.zeros_like(q)
            qq = jnp.concatenate([jnp.where(lane < A_DH, q, zero), jnp.where(lane >= A_DH, q, zero)], axis=0)
            out.append(_nt_dot(k_ref[:kend, lanes], qq))
        return out

    upcoming = None
    for i in range(seq // tq):
        kend = (i + 1) * tq
        scores = upcoming if i else tile_scores(0)
        if i + 1 < seq // tq:
            upcoming = tile_scores(i + 1)
        for hh in range(hps):
            lanes = slice(hh * HEAD_W, (hh + 1) * HEAD_W)
            st = scores[hh]
            st_diag = jnp.where(causal, st[kend - tq:, :], -jnp.inf)
            ot = _softmax_pv_t(st[:kend - tq, :] if i else None, st_diag, v1t_ref.at[hh], kend, None)
            at = ot[:, :tq] - lam * ot[:, tq:]
            ms = jnp.mean(at * at, axis=0, keepdims=True)
            yt = at * lax.rsqrt(ms + SUBLN_EPS) * g_ref[...]
            o_ref[i * tq:kend, lanes] = (yt * (1.0 - lambda_init)).T.astype(o_ref.dtype)


def diff_attention(proj, diff_lambda, subln_g, *, batch, seq, heads, col0, lambda_init, tq=256, hps=2):
    tq = min(tq, seq)
    hps = min(hps, heads)
    kern = functools.partial(_diffattn_kernel, seq=seq, tq=tq, lambda_init=lambda_init)
    hblocks = heads // hps

    def head(k):
        return pl.BlockSpec((seq, hps * HEAD_W), lambda b, h: (b, (col0 + k * heads) // hps + h))

    return pl.pallas_call(
        kern,
        grid=(batch, hblocks),
        in_specs=[pl.BlockSpec((4, A_DH), lambda b, h: (0, 0)),
                  pl.BlockSpec((HEAD_W, 1), lambda b, h: (0, 0)),
                  head(0), head(1), head(2)],
        out_specs=pl.BlockSpec((seq, hps * HEAD_W), lambda b, h: (b, h)),
        out_shape=jax.ShapeDtypeStruct((batch * seq, heads * HEAD_W), BF16),
        scratch_shapes=[pltpu.VMEM((hps, V1T_ROWS, seq), BF16)],
        compiler_params=_params("arbitrary", "arbitrary"),
        name="diff_attention",
    )(diff_lambda, subln_g.reshape(HEAD_W, 1), proj, proj, proj)


def _conv_kernel(b_ref, c_ref, h_ref, w_ref, o_ref):
    u = c_ref[...].astype(F32) * h_ref[...].astype(F32)
    row = lax.broadcasted_iota(jnp.int32, u.shape, 0)
    w = w_ref[...]
    acc = w[B_CONV - 1:B_CONV] * u
    for d in range(1, B_CONV):
        shifted = jnp.where(row >= d, pltpu.roll(u, d, axis=0), 0.0)
        acc = acc + w[B_CONV - 1 - d:B_CONV - d] * shifted
    o_ref[...] = (b_ref[...].astype(F32) * acc).astype(o_ref.dtype)


def short_conv_gate(proj, conv_w, *, batch, seq, width, col0, cw=256):
    cw = min(cw, width)
    nc = width // cw
    c0 = col0 // cw

    def col(k):
        return pl.BlockSpec((seq, cw), lambda b, j: (b, c0 + k * nc + j))

    return pl.pallas_call(
        _conv_kernel,
        grid=(batch, nc),
        in_specs=[col(0), col(1), col(2), pl.BlockSpec((B_CONV, cw), lambda b, j: (0, j))],
        out_specs=pl.BlockSpec((seq, cw), lambda b, j: (b, j)),
        out_shape=jax.ShapeDtypeStruct((batch * seq, width), BF16),
        compiler_params=_params("arbitrary", "arbitrary"),
        name="short_conv_gate",
    )(proj, proj, proj, conv_w)


def _moba_kernel(q_ref, k_ref, v_ref, o_ref, v1_ref, ka_ref, *, seq):
    blk = C_BLOCK
    nb = seq // blk
    nbp = 2 * SUBLANES
    assert nb <= nbp
    hps = v1_ref.shape[0]
    key_blk = lax.broadcasted_iota(jnp.int32, (seq, HEAD_W), 0) // blk
    indicator = jnp.where(key_blk == lax.broadcasted_iota(jnp.int32, (seq, HEAD_W), 1), 1.0, 0.0).astype(BF16)
    rowid = lax.broadcasted_iota(jnp.int32, (nbp, HEAD_W), 0)
    km_hi, km_lo = [], []
    for hh in range(hps):
        lanes = slice(hh * HEAD_W, (hh + 1) * HEAD_W)
        v1_ref[hh, :, :HEAD_W] = v_ref[:, lanes]
        v1_ref[hh, :, HEAD_W:] = jnp.ones((seq, HEAD_W), BF16)
        ka_ref[hh, :, :HEAD_W] = k_ref[:, lanes]
        ka_ref[hh, :, HEAD_W:] = indicator
        km = jnp.zeros((nbp, HEAD_W), F32)
        for n in range(nb):
            mean_n = jnp.mean(k_ref[n * blk:(n + 1) * blk, lanes].astype(F32), axis=0, keepdims=True)
            km = jnp.where(rowid == n, mean_n, km)
        km_hi.append(km.astype(BF16))
        km_lo.append((km - km_hi[hh].astype(F32)).astype(BF16))

    kpos = lax.broadcasted_iota(jnp.int32, (blk, blk), 1)
    causal = kpos <= lax.broadcasted_iota(jnp.int32, (blk, blk), 0)
    blkid = lax.broadcasted_iota(jnp.int32, (nbp, blk), 0)
    exp_scale = HEAD_W ** -0.5 * math.log2(math.e)

    def tile_scores(i):
        kend = (i + 1) * blk
        scores = []
        for hh in range(hps):
            lanes = slice(hh * HEAD_W, (hh + 1) * HEAD_W)
            q = q_ref[i * blk:kend, lanes]
            if i <= C_TOPK:
                scores.append(_nt_dot(q, k_ref[:kend, lanes]))
                continue
            gate = _nt_dot(km_hi[hh], q) + _nt_dot(km_lo[hh], q)
            cnt = jnp.zeros(gate.shape, jnp.int32)
            for m in range(i):
                gm = gate[m:m + 1, :]
                cnt = cnt + jnp.where(gm > gate, 1, jnp.where((gm == gate) & (m < blkid), 1, 0))
            keep = ((blkid < i) & (cnt < C_TOPK)) | (blkid == i)
            bias_t = jnp.where(keep, 0.0, NEG_BIAS)
            pad = jnp.zeros((HEAD_W - nbp, blk), F32)
            bias = jnp.concatenate([bias_t, pad], axis=0).T.astype(BF16)
            scores.append(_nt_dot(jnp.concatenate([q, bias], axis=1), ka_ref[hh, :kend, :]))
        return scores

    upcoming = None
    for i in range(nb):
        kend = (i + 1) * blk
        scores = upcoming if i else tile_scores(0)
        if i + 1 < nb:
            upcoming = tile_scores(i + 1)
        for hh in range(hps):
            s = scores[hh]
            s_diag = jnp.where(causal, s[:, kend - blk:], -jnp.inf)
            o = _softmax_pv(s[:, :kend - blk] if i else None, s_diag, v1_ref.at[hh], kend, exp_scale)
            o_ref[i * blk:kend, hh * HEAD_W:(hh + 1) * HEAD_W] = o.astype(o_ref.dtype)


def moba_attention(proj, *, batch, seq, heads, col0, hps=2):
    hps = min(hps, heads)
    kern = functools.partial(_moba_kernel, seq=seq)

    def head(k):
        return pl.BlockSpec((seq, hps * HEAD_W), lambda b, h: (b, (col0 + k * heads) // hps + h))

    return pl.pallas_call(
        kern,
        grid=(batch, heads // hps),
        in_specs=[head(0), head(1), head(2)],
        out_specs=pl.BlockSpec((seq, hps * HEAD_W), lambda b, h: (b, h)),
        out_shape=jax.ShapeDtypeStruct((batch * seq, heads * HEAD_W), BF16),
        scratch_shapes=[pltpu.VMEM((hps, seq, 2 * HEAD_W), BF16), pltpu.VMEM((hps, seq, 2 * HEAD_W), BF16)],
        compiler_params=_params("arbitrary", "arbitrary"),
        name="moba_attention",
    )(proj, proj, proj)


def _branch_kernel(a_ref, b_ref, c_ref, ga_ref, gb_ref, gc_ref, w_ref, o_ref, wbf_ref):
    @pl.when(pl.program_id(1) == 0)
    def _():
        for g in range(3):
            _cast_rows(w_ref.at[g], wbf_ref.at[g], 256)

    acc = None
    for g, (x_ref, gl_ref) in enumerate(((a_ref, ga_ref), (b_ref, gb_ref), (c_ref, gc_ref))):
        br = jnp.dot(x_ref[...], wbf_ref[g], preferred_element_type=F32)
        term = jax.nn.sigmoid(gl_ref[...].astype(F32)) * br
        acc = term if acc is None else acc + term
    o_ref[...] = acc.astype(o_ref.dtype)


def branch_merge(out_a, out_b, out_c, proj, w_branch, layer, *, gate_col0, tm=512, tn=512):
    M, W = out_a.shape
    D = w_branch.shape[3]
    tm, tn = min(tm, M), min(tn, D)
    g0 = gate_col0 // tn
    gstride = D // tn
    x_spec = pl.BlockSpec((tm, W), lambda j, i: (i, 0))

    def gate_spec(g):
        return pl.BlockSpec((tm, tn), lambda j, i: (i, g0 + g * gstride + j))

    return pl.pallas_call(
        _branch_kernel,
        grid=(D // tn, M // tm),
        in_specs=[x_spec, x_spec, x_spec, gate_spec(0), gate_spec(1), gate_spec(2),
                  pl.BlockSpec((None, 3, W, tn), lambda j, i: (layer, 0, 0, j))],
        out_specs=pl.BlockSpec((tm, tn), lambda j, i: (i, j)),
        out_shape=jax.ShapeDtypeStruct((M, D), BF16),
        scratch_shapes=[pltpu.VMEM((3, W, tn), BF16)],
        compiler_params=_params("arbitrary", "arbitrary"),
        name="branch_merge",
    )(out_a, out_b, out_c, proj, proj, proj, w_branch)


def _rank_rows(v, n):
    rowid = lax.broadcasted_iota(jnp.int32, v.shape, 0)
    cnt = jnp.zeros(v.shape, jnp.int32)
    for m in range(n):
        vm = v[m:m + 1, :]
        cnt = cnt + jnp.where(vm > v, 1, jnp.where((vm == v) & (m < rowid), 1, 0))
    return cnt


def _route(logits_t, bias):
    E = logits_t.shape[0]
    gsz = E // N_GROUPS
    scores = jax.nn.sigmoid(logits_t)
    choice = scores + bias
    ridx = lax.broadcasted_iota(jnp.int32, (gsz, logits_t.shape[1]), 0).astype(F32)
    gid = lax.broadcasted_iota(jnp.int32, (N_GROUPS, logits_t.shape[1]), 0)
    gscore = jnp.zeros((N_GROUPS, logits_t.shape[1]), F32)
    for g in range(N_GROUPS):
        cg = choice[g * gsz:(g + 1) * gsz, :]
        m1 = jnp.max(cg, axis=0, keepdims=True)
        first = jnp.min(jnp.where(cg == m1, ridx, float(gsz)), axis=0, keepdims=True)
        m2 = jnp.max(jnp.where(ridx == first, -jnp.inf, cg), axis=0, keepdims=True)
        gscore = jnp.where(gid == g, m1 + m2, gscore)
    gsel = _rank_rows(gscore, N_GROUPS) < TOPK_GROUPS
    masked = jnp.concatenate(
        [jnp.where(gsel[g:g + 1, :], choice[g * gsz:(g + 1) * gsz, :], -jnp.inf) for g in range(N_GROUPS)],
        axis=0)
    sel = _rank_rows(masked, E) < TOP_K
    w = jnp.where(sel, scores, 0.0)
    return sel, w / jnp.sum(w, axis=0, keepdims=True) * ROUTED_SCALE


def _pack_pairs(v):
    half = v.shape[1] // 2
    lo = lax.bitcast_convert_type(v[:, :half], jnp.uint32) >> 16
    hi = lax.bitcast_convert_type(v[:, half:], jnp.uint32) & jnp.uint32(0xFFFF0000)
    return hi | lo


def _unpack_pairs(p):
    lo = lax.bitcast_convert_type(p << 16, F32)
    hi = lax.bitcast_convert_type(p & jnp.uint32(0xFFFF0000), F32)
    return lo, hi


def _norm_route_kernel(x_ref, g_ref, sc_ref, sh_ref, rw_ref, rb_ref,
                       hp_ref, e8_ref, p8_ref, w8_ref, cnt_ref, carry_ref):
    @pl.when((pl.program_id(0) == 0) & (pl.program_id(1) == 0))
    def _():
        carry_ref[...] = jnp.zeros_like(carry_ref)

    hf = _modnorm(x_ref[...], g_ref[...], sc_ref[...], sh_ref[...])
    h = hf.astype(BF16)
    hp_ref[...] = _pack_pairs(h.astype(F32))
    h_lo = (hf - h.astype(F32)).astype(BF16)
    rw = rw_ref[...]
    rw_hi = rw.astype(BF16)
    rw_lo = (rw - rw_hi.astype(F32)).astype(BF16)
    logits = _nt_dot(rw_hi, h) + (_nt_dot(rw_hi, h_lo) + _nt_dot(rw_lo, h))
    sel, gates = _route(logits, rb_ref[...])
    E, ts = gates.shape
    self32 = jnp.where(sel, 1.0, 0.0)
    selb = self32.astype(BF16)

    def indicator(cond):
        return jnp.where(cond, 1.0, 0.0).astype(BF16)

    r = lax.broadcasted_iota(jnp.int32, (ts, ts), 0)
    c = lax.broadcasted_iota(jnp.int32, (ts, ts), 1)
    before = jnp.dot(selb, indicator(r < c), preferred_element_type=F32)
    pos = carry_ref[...] + before
    carry_ref[...] += jnp.sum(self32, axis=1, keepdims=True)
    cnt_ref[...] = jnp.broadcast_to(carry_ref[...], cnt_ref.shape)

    er = lax.broadcasted_iota(jnp.int32, (E, E), 0)
    ec = lax.broadcasted_iota(jnp.int32, (E, E), 1)
    slot = jnp.dot(indicator(ec < er), selb, preferred_element_type=F32)
    eid = lax.broadcasted_iota(jnp.int32, (E, ts), 0).astype(F32)
    for k in range(TOP_K):
        mk = sel & (slot == float(k))
        e8_ref[k:k + 1, :] = jnp.sum(jnp.where(mk, eid, 0.0), axis=0, keepdims=True).astype(jnp.int32)
        p8_ref[k:k + 1, :] = jnp.sum(jnp.where(mk, pos, 0.0), axis=0, keepdims=True).astype(jnp.int32)
        w8_ref[k:k + 1, :] = jnp.sum(jnp.where(mk, gates, 0.0), axis=0, keepdims=True)


def norm_route(x, g, sc, sh, router_w, router_bias, *, ts=256):
    B, S, D = x.shape
    E = router_w.shape[1]
    ts = min(ts, S)
    nt = S // ts
    T = B * S
    row = pl.BlockSpec((None, ts, D), lambda b, i: (b, i, 0))
    per_b = pl.BlockSpec((None, 1, D), lambda b, i: (b, 0, 0))
    slots = pl.BlockSpec((TOP_K, ts), lambda b, i: (0, b * nt + i))
    return pl.pallas_call(
        _norm_route_kernel,
        grid=(B, nt),
        in_specs=[row, pl.BlockSpec((1, D), lambda b, i: (0, 0)), per_b, per_b,
                  pl.BlockSpec((E, D), lambda b, i: (0, 0)),
                  pl.BlockSpec((E, 1), lambda b, i: (0, 0))],
        out_specs=[pl.BlockSpec((ts, D // 2), lambda b, i: (b * nt + i, 0)), slots, slots, slots,
                   pl.BlockSpec((E, LANES), lambda b, i: (0, 0))],
        out_shape=[jax.ShapeDtypeStruct((T, D // 2), jnp.uint32),
                   jax.ShapeDtypeStruct((TOP_K, T), jnp.int32),
                   jax.ShapeDtypeStruct((TOP_K, T), jnp.int32),
                   jax.ShapeDtypeStruct((TOP_K, T), F32),
                   jax.ShapeDtypeStruct((E, LANES), F32)],
        scratch_shapes=[pltpu.VMEM((E, 1), F32)],
        compiler_params=_params("arbitrary", "arbitrary"),
        name="norm_route",
    )(x, g.reshape(1, D), sc.reshape(B, 1, D), sh.reshape(B, 1, D),
      router_w.T, router_bias.reshape(E, 1))


def _moe_plan(e8, p8, counts, *, tm):
    E = counts.shape[0]
    P = e8.shape[0] * e8.shape[1]
    ntiles = P // tm
    cnt = counts.astype(jnp.int32)
    off = jnp.cumsum(cnt) - cnt
    onehot = e8[None, :, :] == jnp.arange(E, dtype=jnp.int32)[:, None, None]
    dest = p8 + jnp.sum(jnp.where(onehot, off[:, None, None], 0), axis=0)
    bnd = jnp.sort(jnp.concatenate([jnp.arange(ntiles, dtype=jnp.int32) * tm, off]))
    nxt = jnp.concatenate([bnd[1:], jnp.full((1,), P, jnp.int32)])
    tile = jnp.minimum(bnd // tm, ntiles - 1)
    owner = jnp.sum((off[None, :] <= bnd[:, None]).astype(jnp.int32), axis=1) - 1
    expert = lax.cummax(jnp.where(nxt > bnd, owner, 0), axis=0)
    new_expert = jnp.concatenate([jnp.ones((1,), jnp.int32), (expert[1:] != expert[:-1]).astype(jnp.int32)])
    meta = jnp.stack([tile, expert, bnd, nxt, new_expert])
    return dest, meta


def _swiglu_act(lo, hi, w13_ref, w2_ref):
    half = lo.shape[1]
    f = w2_ref.shape[0]
    ab = (jnp.dot(lo, w13_ref[:half, :], preferred_element_type=F32)
          + jnp.dot(hi, w13_ref[half:, :], preferred_element_type=F32))
    a, b = ab[:, :f], ab[:, f:]
    return (a * jax.nn.sigmoid(a) * b).astype(BF16)


def _dispatch_kernel(dest_ref, h_ref, xs_ref, sem, *, tt):
    def body(t, carry):
        for k in range(TOP_K):
            d = dest_ref[0, t * TOP_K + k]
            pltpu.make_async_copy(h_ref.at[pl.ds(t, 1)], xs_ref.at[pl.ds(d, 1)], sem).start(priority=k % 2)
        return carry

    lax.fori_loop(0, tt, body, 0)
    for k in range(TOP_K):
        pltpu.make_async_copy(h_ref, xs_ref.at[pl.ds(0, tt)], sem).wait()


def moe_dispatch(hp, dest_tok, *, tt=256):
    T, Dh = hp.shape
    tt = min(tt, T)
    kern = functools.partial(_dispatch_kernel, tt=tt)
    return pl.pallas_call(
        kern,
        grid=(T // tt,),
        in_specs=[pl.BlockSpec((None, 1, tt * TOP_K), lambda i: (i, 0, 0), memory_space=pltpu.SMEM),
                  pl.BlockSpec((tt, Dh), lambda i: (i, 0))],
        out_specs=pl.BlockSpec(memory_space=pl.ANY),
        out_shape=jax.ShapeDtypeStruct((T * TOP_K, Dh), jnp.uint32),
        scratch_shapes=[pltpu.SemaphoreType.DMA],
        compiler_params=_params("arbitrary"),
        name="moe_dispatch",
    )(dest_tok.reshape(T // tt, 1, tt * TOP_K), hp)


def _moe_gemm_kernel(meta_ref, xs_ref, w1_ref, w3_ref, w2_ref, o_ref, w13_ref, w2b_ref, acc_ref, *, tm):
    k = pl.program_id(0)
    tile, start, end = meta_ref[0, k], meta_ref[2, k], meta_ref[3, k]
    f = w2_ref.shape[0]

    @pl.when(meta_ref[4, k] == 1)
    def _():
        _cast_rows(w1_ref, w13_ref.at[:, :f], 256)
        _cast_rows(w3_ref, w13_ref.at[:, f:], 256)
        _cast_rows(w2_ref, w2b_ref, 256)

    opens = start == tile * tm
    closes = end == (tile + 1) * tm

    def pack(y):
        return _pack_pairs(y.astype(BF16).astype(F32))

    @pl.when(end > start)
    def _():
        lo, hi = _unpack_pairs(xs_ref[...])
        act = _swiglu_act(lo.astype(BF16), hi.astype(BF16), w13_ref, w2b_ref)
        half = o_ref.shape[1]

        @pl.when(opens & closes)
        def _():
            chunk = min(half, 4 * LANES)
            for c in range(0, half, chunk):
                y_lo = jnp.dot(act, w2b_ref[:, c:c + chunk], preferred_element_type=F32)
                y_hi = jnp.dot(act, w2b_ref[:, half + c:half + c + chunk], preferred_element_type=F32)
                o_ref[:, c:c + chunk] = pack(jnp.concatenate([y_lo, y_hi], axis=1))

        @pl.when(jnp.logical_not(opens & closes))
        def _():
            y = jnp.dot(act, w2b_ref[...], preferred_element_type=F32)
            rows = tile * tm + lax.broadcasted_iota(jnp.int32, (tm, 1), 0)
            mine = jnp.where((rows >= start) & (rows < end), y, 0.0)

            @pl.when(opens)
            def _():
                acc_ref[...] = mine

            @pl.when(jnp.logical_not(opens | closes))
            def _():
                acc_ref[...] += mine

            @pl.when(closes)
            def _():
                o_ref[...] = pack(acc_ref[...] + mine)


def moe_gemm(xs, meta, w1, w3, w2, layer, *, tm):
    P, Dh = xs.shape
    _, E, D, F = w1.shape
    n_items = meta.shape[1]
    kern = functools.partial(_moe_gemm_kernel, tm=tm)
    grid_spec = pltpu.PrefetchScalarGridSpec(
        num_scalar_prefetch=1,
        grid=(n_items,),
        in_specs=[pl.BlockSpec((tm, Dh), lambda k, m: (m[0, k], 0)),
                  pl.BlockSpec((None, None, D, F), lambda k, m: (layer, m[1, k], 0, 0)),
                  pl.BlockSpec((None, None, D, F), lambda k, m: (layer, m[1, k], 0, 0)),
                  pl.BlockSpec((None, None, F, D), lambda k, m: (layer, m[1, k], 0, 0))],
        out_specs=pl.BlockSpec((tm, Dh), lambda k, m: (m[0, k], 0)),
        scratch_shapes=[pltpu.VMEM((D, 2 * F), BF16), pltpu.VMEM((F, D), BF16), pltpu.VMEM((tm, D), F32)],
    )
    return pl.pallas_call(
        kern,
        grid_spec=grid_spec,
        out_shape=jax.ShapeDtypeStruct((P, Dh), jnp.uint32),
        compiler_params=_params("arbitrary"),
        name="moe_gemm",
    )(meta, xs, w1, w3, w2)


def _combine_kernel(dest_ref, ys_ref, w_ref, hp_ref, x_ref, gt_ref, s13_ref, s2_ref,
                    g_ref, sc_ref, sh_ref, xo_ref, ho_ref, buf_ref, sem, *, tt):
    def gather(t, carry):
        for k in range(TOP_K):
            d = dest_ref[0, t * TOP_K + k]
            pltpu.make_async_copy(ys_ref.at[pl.ds(d, 1)], buf_ref.at[k, pl.ds(t, 1)], sem).start(priority=k % 2)
        return carry

    lax.fori_loop(0, tt, gather, 0)

    lo, hi = _unpack_pairs(hp_ref[...])
    y = jnp.dot(_swiglu_act(lo.astype(BF16), hi.astype(BF16), s13_ref, s2_ref), s2_ref[...],
                preferred_element_type=F32)
    for k in range(TOP_K):
        pltpu.make_async_copy(ys_ref.at[pl.ds(0, tt)], buf_ref.at[k], sem).wait()
    w = w_ref[...]
    acc_lo = acc_hi = None
    for k in range(TOP_K):
        lo, hi = _unpack_pairs(buf_ref[k])
        wk = w[:, k:k + 1]
        acc_lo = wk * lo if acc_lo is None else acc_lo + wk * lo
        acc_hi = wk * hi if acc_hi is None else acc_hi + wk * hi
    y = y + jnp.concatenate([acc_lo, acc_hi], axis=1)
    x = x_ref[...] + gt_ref[...] * y
    xo_ref[...] = x
    ho_ref[...] = _modnorm(x, g_ref[...], sc_ref[...], sh_ref[...]).astype(ho_ref.dtype)


def moe_combine(ys, dest_tok, w_tok, hp, x, gate, s13, s2, g, sc, sh, *, out_dtype, tt=128):
    B, S, D = x.shape
    T = B * S
    Dh = hp.shape[1]
    tt = min(tt, S)
    nt = S // tt
    n_steps = T // tt
    row = pl.BlockSpec((tt, D), lambda i: (i, 0))
    per_b = pl.BlockSpec((None, 1, D), lambda i: (i // nt, 0, 0))

    def whole(a):
        return pl.BlockSpec(a.shape, lambda i: (0,) * a.ndim)

    kern = functools.partial(_combine_kernel, tt=tt)
    xo, ho = pl.pallas_call(
        kern,
        grid=(n_steps,),
        in_specs=[pl.BlockSpec((None, 1, tt * TOP_K), lambda i: (i, 0, 0), memory_space=pltpu.SMEM),
                  pl.BlockSpec(memory_space=pl.ANY),
                  pl.BlockSpec((tt, TOP_K), lambda i: (i, 0)),
                  pl.BlockSpec((tt, Dh), lambda i: (i, 0)),
                  row, per_b, whole(s13), whole(s2),
                  pl.BlockSpec((1, D), lambda i: (0, 0)), per_b, per_b],
        out_specs=[row, row],
        out_shape=[jax.ShapeDtypeStruct((T, D), F32), jax.ShapeDtypeStruct((T, D), out_dtype)],
        scratch_shapes=[pltpu.VMEM((TOP_K, tt, Dh), jnp.uint32), pltpu.SemaphoreType.DMA],
        compiler_params=_params("arbitrary"),
        name="moe_combine",
    )(dest_tok.reshape(n_steps, 1, tt * TOP_K), ys, w_tok, hp, x.reshape(T, D), gate.reshape(B, 1, D), s13, s2,
      g.reshape(1, D), sc.reshape(B, 1, D), sh.reshape(B, 1, D))
    return xo.reshape(B, S, D), ho.reshape(B, S, D)


def kernel(x, c, w_ada, ada_table, norm1_g, norm2_g, w_in, diff_lambda, diff_subln_g, conv_w,
           w_branch, w_o, router_w, router_bias, exp_w1, exp_w3, exp_w2,
           shared_w1, shared_w3, shared_w2, final_g):
    B, S, D = x.shape
    L = w_in.shape[0]
    W = w_branch.shape[2]
    heads = W // HEAD_W
    T = B * S
    assert w_in.shape[2] == 9 * W + 3 * D and S % C_BLOCK == 0
    moe_tm = min(MOE_ROW_TILE, T)

    mod = ada_mod(c, w_ada, ada_table)

    def mod_of(l, k):
        return mod[l, :B, k * D:(k + 1) * D]

    zeros_bd = jnp.zeros((B, D), F32)
    h = modnorm(x, norm1_g[0], mod_of(0, 1), mod_of(0, 0))
    for l in range(L):
        proj = matmul_wcast(h.reshape(T, D), w_in, l)
        lambda_init = 0.8 - 0.6 * math.exp(-0.3 * l)
        out_a = diff_attention(proj, diff_lambda[l], diff_subln_g[l], batch=B, seq=S, heads=heads,
                               col0=0, lambda_init=lambda_init)
        out_b = short_conv_gate(proj, conv_w[l], batch=B, seq=S, width=W, col0=3 * W)
        out_c = moba_attention(proj, batch=B, seq=S, heads=heads, col0=6 * heads)
        merged = branch_merge(out_a, out_b, out_c, proj, w_branch, l, gate_col0=9 * W)
        x1 = matmul_residual(merged, w_o, l, x.reshape(T, D), mod_of(l, 2), seq=S).reshape(B, S, D)

        hp, e8, p8, w8, counts = norm_route(x1, norm2_g[l], mod_of(l, 4), mod_of(l, 3),
                                            router_w[l], router_bias[l])
        dest, meta = _moe_plan(e8, p8, counts[:, 0], tm=moe_tm)
        dest_tok = dest.T.reshape(T * TOP_K)
        xs = moe_dispatch(hp, dest_tok)
        ys = moe_gemm(xs, meta, exp_w1, exp_w3, exp_w2, l, tm=moe_tm)
        s13 = jnp.concatenate([shared_w1[l], shared_w3[l]], axis=1).astype(BF16)
        s2 = shared_w2[l].astype(BF16)
        if l + 1 < L:
            norm_args = (norm1_g[l + 1], mod_of(l + 1, 1), mod_of(l + 1, 0))
        else:
            norm_args = (final_g, zeros_bd, zeros_bd)
        x, h = moe_combine(ys, dest_tok, w8.T, hp, x1, mod_of(l, 5), s13, s2, *norm_args,
                           out_dtype=BF16 if l + 1 < L else F32)
    return h
```

```python
import functools
import math

import jax
import jax.numpy as jnp
from jax import lax
from jax.experimental import pallas as pl
from jax.experimental.pallas import tpu as pltpu

F32 = jnp.float32
BF16 = jnp.bfloat16

A_DH = 64
HEAD_W = 128
C_BLOCK = 256
C_TOPK = 3
N_GROUPS = 8
TOPK_GROUPS = 4
TOP_K = 8
ROUTED_SCALE = 2.5
N_MOD = 6
B_CONV = 3
EPS = 1e-6
SUBLN_EPS = 1e-5
NEG_BIAS = -2.0 ** 100

VMEM_LIMIT_BYTES = 60 * 1024 * 1024
LANES = 128
SUBLANES = 8
MOE_ROW_TILE = 256
V1T_ROWS = HEAD_W + 2 * SUBLANES


def _params(*sem, **kw):
    return pltpu.CompilerParams(dimension_semantics=sem, vmem_limit_bytes=VMEM_LIMIT_BYTES, **kw)


def _nt_dot(a, b):
    return lax.dot_general(a, b, (((1,), (1,)), ((), ())), preferred_element_type=F32)


def _cast_rows(src_ref, dst_ref, rows):
    n = src_ref.shape[0]
    rows = min(rows, n)

    def body(r, carry):
        sl = pl.ds(pl.multiple_of(r * rows, rows), rows)
        dst_ref[sl, :] = src_ref[sl, :].astype(dst_ref.dtype)
        return carry

    lax.fori_loop(0, n // rows, body, 0)


def _ada_kernel(c_ref, w_ref, t_ref, o_ref):
    c = c_ref[...]
    a = (c * jax.nn.sigmoid(c)).astype(BF16)
    acc = jnp.dot(a, w_ref[...].astype(BF16), preferred_element_type=F32)
    o_ref[...] = acc[None, :, :] + t_ref[...]


def ada_mod(c, w_ada, ada_table, *, tn=512):
    B, D = c.shape
    L = ada_table.shape[0]
    N = w_ada.shape[1]
    tn = min(tn, N)
    cp = jnp.zeros((SUBLANES, D), F32).at[:B].set(c)
    table = ada_table.reshape(L, 1, N)
    return pl.pallas_call(
        _ada_kernel,
        grid=(N // tn,),
        in_specs=[pl.BlockSpec((SUBLANES, D), lambda j: (0, 0)),
                  pl.BlockSpec((D, tn), lambda j: (0, j)),
                  pl.BlockSpec((L, 1, tn), lambda j: (0, 0, j))],
        out_specs=pl.BlockSpec((L, SUBLANES, tn), lambda j: (0, 0, j)),
        out_shape=jax.ShapeDtypeStruct((L, SUBLANES, N), F32),
        compiler_params=_params("arbitrary"),
        name="ada_mod",
    )(cp, w_ada, table)


def _modnorm(x, g, sc, sh):
    ms = jnp.mean(x * x, axis=-1, keepdims=True)
    y = x * lax.rsqrt(ms + EPS) * g
    return y * (1.0 + sc) + sh


def _modnorm_kernel(x_ref, g_ref, sc_ref, sh_ref, o_ref):
    o_ref[...] = _modnorm(x_ref[...], g_ref[...], sc_ref[...], sh_ref[...]).astype(o_ref.dtype)


def modnorm(x, g, sc, sh, *, ts=256):
    B, S, D = x.shape
    ts = min(ts, S)
    row = pl.BlockSpec((None, ts, D), lambda b, i: (b, i, 0))
    per_b = pl.BlockSpec((None, 1, D), lambda b, i: (b, 0, 0))
    return pl.pallas_call(
        _modnorm_kernel,
        grid=(B, S // ts),
        in_specs=[row, pl.BlockSpec((1, D), lambda b, i: (0, 0)), per_b, per_b],
        out_specs=row,
        out_shape=jax.ShapeDtypeStruct((B, S, D), BF16),
        compiler_params=_params("arbitrary", "arbitrary"),
        name="modnorm",
    )(x, g.reshape(1, D), sc.reshape(B, 1, D), sh.reshape(B, 1, D))


def _mm_kernel(x_ref, w_ref, o_ref, wbf_ref):
    @pl.when(pl.program_id(1) == 0)
    def _():
        _cast_rows(w_ref, wbf_ref, 256)

    o_ref[...] = jnp.dot(x_ref[...], wbf_ref[...], preferred_element_type=F32).astype(o_ref.dtype)


def matmul_wcast(x, w, layer, *, tm=1024, tn=768, out_dtype=BF16):
    M, K = x.shape
    N = w.shape[2]
    tm, tn = min(tm, M), min(tn, N)
    return pl.pallas_call(
        _mm_kernel,
        grid=(N // tn, M // tm),
        in_specs=[pl.BlockSpec((tm, K), lambda j, i: (i, 0)),
                  pl.BlockSpec((None, K, tn), lambda j, i: (layer, 0, j))],
        out_specs=pl.BlockSpec((tm, tn), lambda j, i: (i, j)),
        out_shape=jax.ShapeDtypeStruct((M, N), out_dtype),
        scratch_shapes=[pltpu.VMEM((K, tn), BF16)],
        compiler_params=_params("arbitrary", "arbitrary"),
        name="matmul_wcast",
    )(x, w)


def _mm_res_kernel(a_ref, w_ref, x_ref, g_ref, o_ref, wbf_ref):
    @pl.when(pl.program_id(1) == 0)
    def _():
        _cast_rows(w_ref, wbf_ref, 256)

    acc = jnp.dot(a_ref[...], wbf_ref[...], preferred_element_type=F32)
    o_ref[...] = x_ref[...] + g_ref[...] * acc


def matmul_residual(a, w, layer, x, gate, *, seq, tm=512, tn=512):
    M, K = a.shape
    N = w.shape[2]
    B = gate.shape[0]
    tm, tn = min(tm, seq), min(tn, N)
    tiles_per_b = seq // tm
    return pl.pallas_call(
        _mm_res_kernel,
        grid=(N // tn, M // tm),
        in_specs=[pl.BlockSpec((tm, K), lambda j, i: (i, 0)),
                  pl.BlockSpec((None, K, tn), lambda j, i: (layer, 0, j)),
                  pl.BlockSpec((tm, tn), lambda j, i: (i, j)),
                  pl.BlockSpec((None, 1, tn), lambda j, i: (i // tiles_per_b, 0, j))],
        out_specs=pl.BlockSpec((tm, tn), lambda j, i: (i, j)),
        out_shape=jax.ShapeDtypeStruct((M, N), F32),
        scratch_shapes=[pltpu.VMEM((K, tn), BF16)],
        compiler_params=_params("arbitrary", "arbitrary"),
        name="matmul_residual",
    )(a, w, x, gate.reshape(B, 1, N))


def _softmax_pv(s_past, s_diag, v1_ref, kend, exp_scale):
    m = jnp.max(s_diag, axis=-1, keepdims=True)
    if s_past is not None:
        m = jnp.maximum(m, jnp.max(s_past, axis=-1, keepdims=True))

    def prob(s):
        z = s - m
        return (jnp.exp(z) if exp_scale is None else jnp.exp2(z * exp_scale)).astype(BF16)

    p = prob(s_diag) if s_past is None else jnp.concatenate([prob(s_past), prob(s_diag)], axis=1)
    ol = jnp.dot(p, v1_ref[:kend, :], preferred_element_type=F32)
    return ol[:, :HEAD_W] / ol[:, HEAD_W:HEAD_W + 1]


def _softmax_pv_t(st_past, st_diag, v1t_ref, kend, exp_scale):
    m = jnp.max(st_diag, axis=0, keepdims=True)
    if st_past is not None:
        m = jnp.maximum(m, jnp.max(st_past, axis=0, keepdims=True))

    def prob(s):
        z = s - m
        return (jnp.exp(z) if exp_scale is None else jnp.exp2(z * exp_scale)).astype(BF16)

    p = prob(st_diag) if st_past is None else jnp.concatenate([prob(st_past), prob(st_diag)], axis=0)
    ol = jnp.dot(v1t_ref[:, :kend], p, preferred_element_type=F32)
    return ol[:HEAD_W] / ol[HEAD_W:HEAD_W + 1]


def _fill_v1t(v_ref, v1t_ref, seq, blk):
    rid = lax.broadcasted_iota(jnp.int32, (V1T_ROWS - HEAD_W, seq), 0)
    for hh in range(v1t_ref.shape[0]):
        lanes = slice(hh * HEAD_W, (hh + 1) * HEAD_W)
        for n in range(seq // blk):
            cols = slice(n * blk, (n + 1) * blk)
            v1t_ref[hh, :HEAD_W, cols] = v_ref[cols, lanes].astype(F32).T.astype(BF16)
        v1t_ref[hh, HEAD_W:, :] = jnp.where(rid == 0, 1.0, 0.0).astype(BF16)


def _diffattn_kernel(dl_ref, g_ref, q_ref, k_ref, v_ref, o_ref, v1t_ref, *, seq, tq, lambda_init):
    hps = v1t_ref.shape[0]
    dl = dl_ref[...]
    lam = (jnp.exp(jnp.sum(dl[0:1] * dl[1:2], keepdims=True))
           - jnp.exp(jnp.sum(dl[2:3] * dl[3:4], keepdims=True)) + lambda_init)
    _fill_v1t(v_ref, v1t_ref, seq, tq)

    lane = lax.broadcasted_iota(jnp.int32, (tq, HEAD_W), 1)
    qpos = lax.broadcasted_iota(jnp.int32, (tq, 2 * tq), 1)
    qpos = jnp.where(qpos >= tq, qpos - tq, qpos)
    causal = lax.broadcasted_iota(jnp.int32, (tq, 2 * tq), 0) <= qpos

    def tile_scores(i):
        kend = (i + 1) * tq
        out = []
        for hh in range(hps):
            lanes = slice(hh * HEAD_W, (hh + 1) * HEAD_W)
            q = q_ref[i * tq:kend, lanes] * jnp.asarray(A_DH ** -0.5, BF16)
            zero = jnp.zeros_like(q)
            qq = jnp.concatenate([jnp.where(lane < A_DH, q, zero), jnp.where(lane >= A_DH, q, zero)], axis=0)
            out.append(_nt_dot(k_ref[:kend, lanes], qq))
        return out

    upcoming = None
    for i in range(seq // tq):
        kend = (i + 1) * tq
        scores = upcoming if i else tile_scores(0)
        if i + 1 < seq // tq:
            upcoming = tile_scores(i + 1)
        for hh in range(hps):
            lanes = slice(hh * HEAD_W, (hh + 1) * HEAD_W)
            st = scores[hh]
            st_diag = jnp.where(causal, st[kend - tq:, :], -jnp.inf)
            ot = _softmax_pv_t(st[:kend - tq, :] if i else None, st_diag, v1t_ref.at[hh], kend, None)
            at = ot[:, :tq] - lam * ot[:, tq:]
            ms = jnp.mean(at * at, axis=0, keepdims=True)
            yt = at * lax.rsqrt(ms + SUBLN_EPS) * g_ref[...]
            o_ref[i * tq:kend, lanes] = (yt * (1.0 - lambda_init)).T.astype(o_ref.dtype)


def diff_attention(proj, diff_lambda, subln_g, *, batch, seq, heads, col0, lambda_init, tq=256, hps=2):
    tq = min(tq, seq)
    hps = min(hps, heads)
    kern = functools.partial(_diffattn_kernel, seq=seq, tq=tq, lambda_init=lambda_init)
    hblocks = heads // hps

    def head(k):
        return pl.BlockSpec((seq, hps * HEAD_W), lambda b, h: (b, (col0 + k * heads) // hps + h))

    return pl.pallas_call(
        kern,
        grid=(batch, hblocks),
        in_specs=[pl.BlockSpec((4, A_DH), lambda b, h: (0, 0)),
                  pl.BlockSpec((HEAD_W, 1), lambda b, h: (0, 0)),
                  head(0), head(1), head(2)],
        out_specs=pl.BlockSpec((seq, hps * HEAD_W), lambda b, h: (b, h)),
        out_shape=jax.ShapeDtypeStruct((batch * seq, heads * HEAD_W), BF16),
        scratch_shapes=[pltpu.VMEM((hps, V1T_ROWS, seq), BF16)],
        compiler_params=_params("arbitrary", "arbitrary"),
        name="diff_attention",
    )(diff_lambda, subln_g.reshape(HEAD_W, 1), proj, proj, proj)


def _conv_kernel(b_ref, c_ref, h_ref, w_ref, o_ref):
    u = c_ref[...].astype(F32) * h_ref[...].astype(F32)
    row = lax.broadcasted_iota(jnp.int32, u.shape, 0)
    w = w_ref[...]
    acc = w[B_CONV - 1:B_CONV] * u
    for d in range(1, B_CONV):
        shifted = jnp.where(row >= d, pltpu.roll(u, d, axis=0), 0.0)
        acc = acc + w[B_CONV - 1 - d:B_CONV - d] * shifted
    o_ref[...] = (b_ref[...].astype(F32) * acc).astype(o_ref.dtype)


def short_conv_gate(proj, conv_w, *, batch, seq, width, col0, cw=256):
    cw = min(cw, width)
    nc = width // cw
    c0 = col0 // cw

    def col(k):
        return pl.BlockSpec((seq, cw), lambda b, j: (b, c0 + k * nc + j))

    return pl.pallas_call(
        _conv_kernel,
        grid=(batch, nc),
        in_specs=[col(0), col(1), col(2), pl.BlockSpec((B_CONV, cw), lambda b, j: (0, j))],
        out_specs=pl.BlockSpec((seq, cw), lambda b, j: (b, j)),
        out_shape=jax.ShapeDtypeStruct((batch * seq, width), BF16),
        compiler_params=_params("arbitrary", "arbitrary"),
        name="short_conv_gate",
    )(proj, proj, proj, conv_w)


def _moba_kernel(q_ref, k_ref, v_ref, o_ref, v1_ref, ka_ref, *, seq):
    blk = C_BLOCK
    nb = seq // blk
    nbp = 2 * SUBLANES
    assert nb <= nbp
    hps = v1_ref.shape[0]
    key_blk = lax.broadcasted_iota(jnp.int32, (seq, HEAD_W), 0) // blk
    indicator = jnp.where(key_blk == lax.broadcasted_iota(jnp.int32, (seq, HEAD_W), 1), 1.0, 0.0).astype(BF16)
    rowid = lax.broadcasted_iota(jnp.int32, (nbp, HEAD_W), 0)
    km_hi, km_lo = [], []
    for hh in range(hps):
        lanes = slice(hh * HEAD_W, (hh + 1) * HEAD_W)
        v1_ref[hh, :, :HEAD_W] = v_ref[:, lanes]
        v1_ref[hh, :, HEAD_W:] = jnp.ones((seq, HEAD_W), BF16)
        ka_ref[hh, :, :HEAD_W] = k_ref[:, lanes]
        ka_ref[hh, :, HEAD_W:] = indicator
        km = jnp.zeros((nbp, HEAD_W), F32)
        for n in range(nb):
            mean_n = jnp.mean(k_ref[n * blk:(n + 1) * blk, lanes].astype(F32), axis=0, keepdims=True)
            km = jnp.where(rowid == n, mean_n, km)
        km_hi.append(km.astype(BF16))
        km_lo.append((km - km_hi[hh].astype(F32)).astype(BF16))

    kpos = lax.broadcasted_iota(jnp.int32, (blk, blk), 1)
    causal = kpos <= lax.broadcasted_iota(jnp.int32, (blk, blk), 0)
    blkid = lax.broadcasted_iota(jnp.int32, (nbp, blk), 0)
    exp_scale = HEAD_W ** -0.5 * math.log2(math.e)

    def tile_scores(i):
        kend = (i + 1) * blk
        scores = []
        for hh in range(hps):
            lanes = slice(hh * HEAD_W, (hh + 1) * HEAD_W)
            q = q_ref[i * blk:kend, lanes]
            if i <= C_TOPK:
                scores.append(_nt_dot(q, k_ref[:kend, lanes]))
                continue
            gate = _nt_dot(km_hi[hh], q) + _nt_dot(km_lo[hh], q)
            cnt = jnp.zeros(gate.shape, jnp.int32)
            for m in range(i):
                gm = gate[m:m + 1, :]
                cnt = cnt + jnp.where(gm > gate, 1, jnp.where((gm == gate) & (m < blkid), 1, 0))
            keep = ((blkid < i) & (cnt < C_TOPK)) | (blkid == i)
            bias_t = jnp.where(keep, 0.0, NEG_BIAS)
            pad = jnp.zeros((HEAD_W - nbp, blk), F32)
            bias = jnp.concatenate([bias_t, pad], axis=0).T.astype(BF16)
            scores.append(_nt_dot(jnp.concatenate([q, bias], axis=1), ka_ref[hh, :kend, :]))
        return scores

    upcoming = None
    for i in range(nb):
        kend = (i + 1) * blk
        scores = upcoming if i else tile_scores(0)
        if i + 1 < nb:
            upcoming = tile_scores(i + 1)
        for hh in range(hps):
            s = scores[hh]
            s_diag = jnp.where(causal, s[:, kend - blk:], -jnp.inf)
            o = _softmax_pv(s[:, :kend - blk] if i else None, s_diag, v1_ref.at[hh], kend, exp_scale)
            o_ref[i * blk:kend, hh * HEAD_W:(hh + 1) * HEAD_W] = o.astype(o_ref.dtype)


def moba_attention(proj, *, batch, seq, heads, col0, hps=2):
    hps = min(hps, heads)
    kern = functools.partial(_moba_kernel, seq=seq)

    def head(k):
        return pl.BlockSpec((seq, hps * HEAD_W), lambda b, h: (b, (col0 + k * heads) // hps + h))

    return pl.pallas_call(
        kern,
        grid=(batch, heads // hps),
        in_specs=[head(0), head(1), head(2)],
        out_specs=pl.BlockSpec((seq, hps * HEAD_W), lambda b, h: (b, h)),
        out_shape=jax.ShapeDtypeStruct((batch * seq, heads * HEAD_W), BF16),
        scratch_shapes=[pltpu.VMEM((hps, seq, 2 * HEAD_W), BF16), pltpu.VMEM((hps, seq, 2 * HEAD_W), BF16)],
        compiler_params=_params("arbitrary", "arbitrary"),
        name="moba_attention",
    )(proj, proj, proj)


def _branch_kernel(a_ref, b_ref, c_ref, ga_ref, gb_ref, gc_ref, w_ref, o_ref, wbf_ref):
    @pl.when(pl.program_id(1) == 0)
    def _():
        for g in range(3):
            _cast_rows(w_ref.at[g], wbf_ref.at[g], 256)

    acc = None
    for g, (x_ref, gl_ref) in enumerate(((a_ref, ga_ref), (b_ref, gb_ref), (c_ref, gc_ref))):
        br = jnp.dot(x_ref[...], wbf_ref[g], preferred_element_type=F32)
        term = jax.nn.sigmoid(gl_ref[...].astype(F32)) * br
        acc = term if acc is None else acc + term
    o_ref[...] = acc.astype(o_ref.dtype)


def branch_merge(out_a, out_b, out_c, proj, w_branch, layer, *, gate_col0, tm=512, tn=512):
    M, W = out_a.shape
    D = w_branch.shape[3]
    tm, tn = min(tm, M), min(tn, D)
    g0 = gate_col0 // tn
    gstride = D // tn
    x_spec = pl.BlockSpec((tm, W), lambda j, i: (i, 0))

    def gate_spec(g):
        return pl.BlockSpec((tm, tn), lambda j, i: (i, g0 + g * gstride + j))

    return pl.pallas_call(
        _branch_kernel,
        grid=(D // tn, M // tm),
        in_specs=[x_spec, x_spec, x_spec, gate_spec(0), gate_spec(1), gate_spec(2),
                  pl.BlockSpec((None, 3, W, tn), lambda j, i: (layer, 0, 0, j))],
        out_specs=pl.BlockSpec((tm, tn), lambda j, i: (i, j)),
        out_shape=jax.ShapeDtypeStruct((M, D), BF16),
        scratch_shapes=[pltpu.VMEM((3, W, tn), BF16)],
        compiler_params=_params("arbitrary", "arbitrary"),
        name="branch_merge",
    )(out_a, out_b, out_c, proj, proj, proj, w_branch)


def _rank_rows(v, n):
    rowid = lax.broadcasted_iota(jnp.int32, v.shape, 0)
    cnt = jnp.zeros(v.shape, jnp.int32)
    for m in range(n):
        vm = v[m:m + 1, :]
        cnt = cnt + jnp.where(vm > v, 1, jnp.where((vm == v) & (m < rowid), 1, 0))
    return cnt


def _route(logits_t, bias):
    E = logits_t.shape[0]
    gsz = E // N_GROUPS
    scores = jax.nn.sigmoid(logits_t)
    choice = scores + bias
    ridx = lax.broadcasted_iota(jnp.int32, (gsz, logits_t.shape[1]), 0).astype(F32)
    gid = lax.broadcasted_iota(jnp.int32, (N_GROUPS, logits_t.shape[1]), 0)
    gscore = jnp.zeros((N_GROUPS, logits_t.shape[1]), F32)
    for g in range(N_GROUPS):
        cg = choice[g * gsz:(g + 1) * gsz, :]
        m1 = jnp.max(cg, axis=0, keepdims=True)
        first = jnp.min(jnp.where(cg == m1, ridx, float(gsz)), axis=0, keepdims=True)
        m2 = jnp.max(jnp.where(ridx == first, -jnp.inf, cg), axis=0, keepdims=True)
        gscore = jnp.where(gid == g, m1 + m2, gscore)
    gsel = _rank_rows(gscore, N_GROUPS) < TOPK_GROUPS
    masked = jnp.concatenate(
        [jnp.where(gsel[g:g + 1, :], choice[g * gsz:(g + 1) * gsz, :], -jnp.inf) for g in range(N_GROUPS)],
        axis=0)
    sel = _rank_rows(masked, E) < TOP_K
    w = jnp.where(sel, scores, 0.0)
    return sel, w / jnp.sum(w, axis=0, keepdims=True) * ROUTED_SCALE


def _pack_pairs(v):
    half = v.shape[1] // 2
    lo = lax.bitcast_convert_type(v[:, :half], jnp.uint32) >> 16
    hi = lax.bitcast_convert_type(v[:, half:], jnp.uint32) & jnp.uint32(0xFFFF0000)
    return hi | lo


def _unpack_pairs(p):
    lo = lax.bitcast_convert_type(p << 16, F32)
    hi = lax.bitcast_convert_type(p & jnp.uint32(0xFFFF0000), F32)
    return lo, hi


def _norm_route_kernel(x_ref, g_ref, sc_ref, sh_ref, rw_ref, rb_ref,
                       hp_ref, e8_ref, p8_ref, w8_ref, cnt_ref, carry_ref):
    @pl.when((pl.program_id(0) == 0) & (pl.program_id(1) == 0))
    def _():
        carry_ref[...] = jnp.zeros_like(carry_ref)

    hf = _modnorm(x_ref[...], g_ref[...], sc_ref[...], sh_ref[...])
    h = hf.astype(BF16)
    hp_ref[...] = _pack_pairs(h.astype(F32))
    h_lo = (hf - h.astype(F32)).astype(BF16)
    rw = rw_ref[...]
    rw_hi = rw.astype(BF16)
    rw_lo = (rw - rw_hi.astype(F32)).astype(BF16)
    logits = _nt_dot(rw_hi, h) + (_nt_dot(rw_hi, h_lo) + _nt_dot(rw_lo, h))
    sel, gates = _route(logits, rb_ref[...])
    E, ts = gates.shape
    self32 = jnp.where(sel, 1.0, 0.0)
    selb = self32.astype(BF16)

    def indicator(cond):
        return jnp.where(cond, 1.0, 0.0).astype(BF16)

    r = lax.broadcasted_iota(jnp.int32, (ts, ts), 0)
    c = lax.broadcasted_iota(jnp.int32, (ts, ts), 1)
    before = jnp.dot(selb, indicator(r < c), preferred_element_type=F32)
    pos = carry_ref[...] + before
    carry_ref[...] += jnp.sum(self32, axis=1, keepdims=True)
    cnt_ref[...] = jnp.broadcast_to(carry_ref[...], cnt_ref.shape)

    er = lax.broadcasted_iota(jnp.int32, (E, E), 0)
    ec = lax.broadcasted_iota(jnp.int32, (E, E), 1)
    slot = jnp.dot(indicator(ec < er), selb, preferred_element_type=F32)
    eid = lax.broadcasted_iota(jnp.int32, (E, ts), 0).astype(F32)
    for k in range(TOP_K):
        mk = sel & (slot == float(k))
        e8_ref[k:k + 1, :] = jnp.sum(jnp.where(mk, eid, 0.0), axis=0, keepdims=True).astype(jnp.int32)
        p8_ref[k:k + 1, :] = jnp.sum(jnp.where(mk, pos, 0.0), axis=0, keepdims=True).astype(jnp.int32)
        w8_ref[k:k + 1, :] = jnp.sum(jnp.where(mk, gates, 0.0), axis=0, keepdims=True)


def norm_route(x, g, sc, sh, router_w, router_bias, *, ts=256):
    B, S, D = x.shape
    E = router_w.shape[1]
    ts = min(ts, S)
    nt = S // ts
    T = B * S
    row = pl.BlockSpec((None, ts, D), lambda b, i: (b, i, 0))
    per_b = pl.BlockSpec((None, 1, D), lambda b, i: (b, 0, 0))
    slots = pl.BlockSpec((TOP_K, ts), lambda b, i: (0, b * nt + i))
    return pl.pallas_call(
        _norm_route_kernel,
        grid=(B, nt),
        in_specs=[row, pl.BlockSpec((1, D), lambda b, i: (0, 0)), per_b, per_b,
                  pl.BlockSpec((E, D), lambda b, i: (0, 0)),
                  pl.BlockSpec((E, 1), lambda b, i: (0, 0))],
        out_specs=[pl.BlockSpec((ts, D // 2), lambda b, i: (b * nt + i, 0)), slots, slots, slots,
                   pl.BlockSpec((E, LANES), lambda b, i: (0, 0))],
        out_shape=[jax.ShapeDtypeStruct((T, D // 2), jnp.uint32),
                   jax.ShapeDtypeStruct((TOP_K, T), jnp.int32),
                   jax.ShapeDtypeStruct((TOP_K, T), jnp.int32),
                   jax.ShapeDtypeStruct((TOP_K, T), F32),
                   jax.ShapeDtypeStruct((E, LANES), F32)],
        scratch_shapes=[pltpu.VMEM((E, 1), F32)],
        compiler_params=_params("arbitrary", "arbitrary"),
        name="norm_route",
    )(x, g.reshape(1, D), sc.reshape(B, 1, D), sh.reshape(B, 1, D),
      router_w.T, router_bias.reshape(E, 1))


def _moe_plan(e8, p8, counts, *, tm):
    E = counts.shape[0]
    P = e8.shape[0] * e8.shape[1]
    ntiles = P // tm
    cnt = counts.astype(jnp.int32)
    off = jnp.cumsum(cnt) - cnt
    onehot = e8[None, :, :] == jnp.arange(E, dtype=jnp.int32)[:, None, None]
    dest = p8 + jnp.sum(jnp.where(onehot, off[:, None, None], 0), axis=0)
    bnd = jnp.sort(jnp.concatenate([jnp.arange(ntiles, dtype=jnp.int32) * tm, off]))
    nxt = jnp.concatenate([bnd[1:], jnp.full((1,), P, jnp.int32)])
    tile = jnp.minimum(bnd // tm, ntiles - 1)
    owner = jnp.sum((off[None, :] <= bnd[:, None]).astype(jnp.int32), axis=1) - 1
    expert = lax.cummax(jnp.where(nxt > bnd, owner, 0), axis=0)
    new_expert = jnp.concatenate([jnp.ones((1,), jnp.int32), (expert[1:] != expert[:-1]).astype(jnp.int32)])
    n = bnd.shape[0]
    slot = (jnp.cumsum(new_expert) - 1) % 2
    change_at = jnp.where(new_expert == 1, jnp.arange(n, dtype=jnp.int32), n)
    next_change = jnp.concatenate([lax.cummin(change_at, axis=0, reverse=True)[1:], jnp.full((1,), n, jnp.int32)])
    has_next = (next_change < n).astype(jnp.int32)
    next_expert = expert[jnp.minimum(next_change, n - 1)]
    meta = jnp.stack([tile, expert, bnd, nxt, new_expert, slot.astype(jnp.int32), has_next, next_expert])
    return dest, meta


def _swiglu_act(lo, hi, w13_ref, w2_ref):
    half = lo.shape[1]
    f = w2_ref.shape[0]
    ab = (jnp.dot(lo, w13_ref[:half, :], preferred_element_type=F32)
          + jnp.dot(hi, w13_ref[half:, :], preferred_element_type=F32))
    a, b = ab[:, :f], ab[:, f:]
    return (a * jax.nn.sigmoid(a) * b).astype(BF16)


def _dispatch_kernel(dest_ref, h_ref, xs_ref, sem, *, tt):
    def body(t, carry):
        for k in range(TOP_K):
            d = dest_ref[0, t * TOP_K + k]
            pltpu.make_async_copy(h_ref.at[pl.ds(t, 1)], xs_ref.at[pl.ds(d, 1)], sem).start(priority=k % 2)
        return carry

    lax.fori_loop(0, tt, body, 0)
    for k in range(TOP_K):
        pltpu.make_async_copy(h_ref, xs_ref.at[pl.ds(0, tt)], sem).wait()


def moe_dispatch(hp, dest_tok, *, tt=256):
    T, Dh = hp.shape
    tt = min(tt, T)
    kern = functools.partial(_dispatch_kernel, tt=tt)
    return pl.pallas_call(
        kern,
        grid=(T // tt,),
        in_specs=[pl.BlockSpec((None, 1, tt * TOP_K), lambda i: (i, 0, 0), memory_space=pltpu.SMEM),
                  pl.BlockSpec((tt, Dh), lambda i: (i, 0))],
        out_specs=pl.BlockSpec(memory_space=pl.ANY),
        out_shape=jax.ShapeDtypeStruct((T * TOP_K, Dh), jnp.uint32),
        scratch_shapes=[pltpu.SemaphoreType.DMA],
        compiler_params=_params("arbitrary"),
        name="moe_dispatch",
    )(dest_tok.reshape(T // tt, 1, tt * TOP_K), hp)


def _moe_gemm_kernel(meta_ref, xs_ref, w1_ref, w3_ref, w2_ref, o_ref,
                     w1s_ref, w3s_ref, w2s_ref, w13_ref, w2b_ref, acc_ref, sem, *, tm, layer):
    k = pl.program_id(0)
    tile, start, end = meta_ref[0, k], meta_ref[2, k], meta_ref[3, k]
    f = w2s_ref.shape[1]

    def stage(expert, slot):
        return (pltpu.make_async_copy(w1_ref.at[layer, expert], w1s_ref.at[slot], sem.at[slot]),
                pltpu.make_async_copy(w3_ref.at[layer, expert], w3s_ref.at[slot], sem.at[slot]),
                pltpu.make_async_copy(w2_ref.at[layer, expert], w2s_ref.at[slot], sem.at[slot]))

    @pl.when(meta_ref[4, k] == 1)
    def _():
        expert, slot = meta_ref[1, k], meta_ref[5, k]

        @pl.when(k == 0)
        def _():
            for cp in stage(expert, slot):
                cp.start()

        for cp in stage(expert, slot):
            cp.wait()

        @pl.when(meta_ref[6, k] == 1)
        def _():
            for cp in stage(meta_ref[7, k], 1 - slot):
                cp.start()

        _cast_rows(w1s_ref.at[slot], w13_ref.at[:, :f], 256)
        _cast_rows(w3s_ref.at[slot], w13_ref.at[:, f:], 256)
        _cast_rows(w2s_ref.at[slot], w2b_ref, 256)

    opens = start == tile * tm
    closes = end == (tile + 1) * tm

    def pack(y):
        return _pack_pairs(y.astype(BF16).astype(F32))

    @pl.when(end > start)
    def _():
        lo, hi = _unpack_pairs(xs_ref[...])
        act = _swiglu_act(lo.astype(BF16), hi.astype(BF16), w13_ref, w2b_ref)
        half = o_ref.shape[1]

        @pl.when(opens & closes)
        def _():
            chunk = min(half, 4 * LANES)
            for c in range(0, half, chunk):
                y_lo = jnp.dot(act, w2b_ref[:, c:c + chunk], preferred_element_type=F32)
                y_hi = jnp.dot(act, w2b_ref[:, half + c:half + c + chunk], preferred_element_type=F32)
                o_ref[:, c:c + chunk] = pack(jnp.concatenate([y_lo, y_hi], axis=1))

        @pl.when(jnp.logical_not(opens & closes))
        def _():
            y = jnp.dot(act, w2b_ref[...], preferred_element_type=F32)
            rows = tile * tm + lax.broadcasted_iota(jnp.int32, (tm, 1), 0)
            mine = jnp.where((rows >= start) & (rows < end), y, 0.0)

            @pl.when(opens)
            def _():
                acc_ref[...] = mine

            @pl.when(jnp.logical_not(opens | closes))
            def _():
                acc_ref[...] += mine

            @pl.when(closes)
            def _():
                o_ref[...] = pack(acc_ref[...] + mine)


def moe_gemm(xs, meta, w1, w3, w2, layer, *, tm):
    P, Dh = xs.shape
    _, E, D, F = w1.shape
    n_items = meta.shape[1]
    kern = functools.partial(_moe_gemm_kernel, tm=tm, layer=layer)
    hbm = pl.BlockSpec(memory_space=pl.ANY)
    grid_spec = pltpu.PrefetchScalarGridSpec(
        num_scalar_prefetch=1,
        grid=(n_items,),
        in_specs=[pl.BlockSpec((tm, Dh), lambda k, m: (m[0, k], 0)), hbm, hbm, hbm],
        out_specs=pl.BlockSpec((tm, Dh), lambda k, m: (m[0, k], 0)),
        scratch_shapes=[pltpu.VMEM((2, D, F), F32), pltpu.VMEM((2, D, F), F32), pltpu.VMEM((2, F, D), F32),
                        pltpu.VMEM((D, 2 * F), BF16), pltpu.VMEM((F, D), BF16), pltpu.VMEM((tm, D), F32),
                        pltpu.SemaphoreType.DMA((2,))],
    )
    return pl.pallas_call(
        kern,
        grid_spec=grid_spec,
        out_shape=jax.ShapeDtypeStruct((P, Dh), jnp.uint32),
        compiler_params=_params("arbitrary"),
        name="moe_gemm",
    )(meta, xs, w1, w3, w2)


def _combine_kernel(dest_ref, ys_ref, w_ref, hp_ref, x_ref, gt_ref, s13_ref, s2_ref,
                    g_ref, sc_ref, sh_ref, xo_ref, ho_ref, buf_ref, sem, *, tt):
    def gather(t, carry):
        for k in range(TOP_K):
            d = dest_ref[0, t * TOP_K + k]
            pltpu.make_async_copy(ys_ref.at[pl.ds(d, 1)], buf_ref.at[k, pl.ds(t, 1)], sem).start(priority=k % 2)
        return carry

    lax.fori_loop(0, tt, gather, 0)

    lo, hi = _unpack_pairs(hp_ref[...])
    y = jnp.dot(_swiglu_act(lo.astype(BF16), hi.astype(BF16), s13_ref, s2_ref), s2_ref[...],
                preferred_element_type=F32)
    for k in range(TOP_K):
        pltpu.make_async_copy(ys_ref.at[pl.ds(0, tt)], buf_ref.at[k], sem).wait()
    w = w_ref[...]
    acc_lo = acc_hi = None
    for k in range(TOP_K):
        lo, hi = _unpack_pairs(buf_ref[k])
        wk = w[:, k:k + 1]
        acc_lo = wk * lo if acc_lo is None else acc_lo + wk * lo
        acc_hi = wk * hi if acc_hi is None else acc_hi + wk * hi
    y = y + jnp.concatenate([acc_lo, acc_hi], axis=1)
    x = x_ref[...] + gt_ref[...] * y
    xo_ref[...] = x
    ho_ref[...] = _modnorm(x, g_ref[...], sc_ref[...], sh_ref[...]).astype(ho_ref.dtype)


def moe_combine(ys, dest_tok, w_tok, hp, x, gate, s13, s2, g, sc, sh, *, out_dtype, tt=128):
    B, S, D = x.shape
    T = B * S
    Dh = hp.shape[1]
    tt = min(tt, S)
    nt = S // tt
    n_steps = T // tt
    row = pl.BlockSpec((tt, D), lambda i: (i, 0))
    per_b = pl.BlockSpec((None, 1, D), lambda i: (i // nt, 0, 0))

    def whole(a):
        return pl.BlockSpec(a.shape, lambda i: (0,) * a.ndim)

    kern = functools.partial(_combine_kernel, tt=tt)
    xo, ho = pl.pallas_call(
        kern,
        grid=(n_steps,),
        in_specs=[pl.BlockSpec((None, 1, tt * TOP_K), lambda i: (i, 0, 0), memory_space=pltpu.SMEM),
                  pl.BlockSpec(memory_space=pl.ANY),
                  pl.BlockSpec((tt, TOP_K), lambda i: (i, 0)),
                  pl.BlockSpec((tt, Dh), lambda i: (i, 0)),
                  row, per_b, whole(s13), whole(s2),
                  pl.BlockSpec((1, D), lambda i: (0, 0)), per_b, per_b],
        out_specs=[row, row],
        out_shape=[jax.ShapeDtypeStruct((T, D), F32), jax.ShapeDtypeStruct((T, D), out_dtype)],
        scratch_shapes=[pltpu.VMEM((TOP_K, tt, Dh), jnp.uint32), pltpu.SemaphoreType.DMA],
        compiler_params=_params("arbitrary"),
        name="moe_combine",
    )(dest_tok.reshape(n_steps, 1, tt * TOP_K), ys, w_tok, hp, x.reshape(T, D), gate.reshape(B, 1, D), s13, s2,
      g.reshape(1, D), sc.reshape(B, 1, D), sh.reshape(B, 1, D))
    return xo.reshape(B, S, D), ho.reshape(B, S, D)


def kernel(x, c, w_ada, ada_table, norm1_g, norm2_g, w_in, diff_lambda, diff_subln_g, conv_w,
           w_branch, w_o, router_w, router_bias, exp_w1, exp_w3, exp_w2,
           shared_w1, shared_w3, shared_w2, final_g):
    B, S, D = x.shape
    L = w_in.shape[0]
    W = w_branch.shape[2]
    heads = W // HEAD_W
    T = B * S
    assert w_in.shape[2] == 9 * W + 3 * D and S % C_BLOCK == 0
    moe_tm = min(MOE_ROW_TILE, T)

    mod = ada_mod(c, w_ada, ada_table)

    def mod_of(l, k):
        return mod[l, :B, k * D:(k + 1) * D]

    zeros_bd = jnp.zeros((B, D), F32)
    h = modnorm(x, norm1_g[0], mod_of(0, 1), mod_of(0, 0))
    for l in range(L):
        proj = matmul_wcast(h.reshape(T, D), w_in, l)
        lambda_init = 0.8 - 0.6 * math.exp(-0.3 * l)
        out_a = diff_attention(proj, diff_lambda[l], diff_subln_g[l], batch=B, seq=S, heads=heads,
                               col0=0, lambda_init=lambda_init)
        out_b = short_conv_gate(proj, conv_w[l], batch=B, seq=S, width=W, col0=3 * W)
        out_c = moba_attention(proj, batch=B, seq=S, heads=heads, col0=6 * heads)
        merged = branch_merge(out_a, out_b, out_c, proj, w_branch, l, gate_col0=9 * W)
        x1 = matmul_residual(merged, w_o, l, x.reshape(T, D), mod_of(l, 2), seq=S).reshape(B, S, D)

        hp, e8, p8, w8, counts = norm_route(x1, norm2_g[l], mod_of(l, 4), mod_of(l, 3),
                                            router_w[l], router_bias[l])
        dest, meta = _moe_plan(e8, p8, counts[:, 0], tm=moe_tm)
        dest_tok = dest.T.reshape(T * TOP_K)
        xs = moe_dispatch(hp, dest_tok)
        ys = moe_gemm(xs, meta, exp_w1, exp_w3, exp_w2, l, tm=moe_tm)
        s13 = jnp.concatenate([shared_w1[l], shared_w3[l]], axis=1).astype(BF16)
        s2 = shared_w2[l].astype(BF16)
        if l + 1 < L:
            norm_args = (norm1_g[l + 1], mod_of(l + 1, 1), mod_of(l + 1, 0))
        else:
            norm_args = (final_g, zeros_bd, zeros_bd)
        x, h = moe_combine(ys, dest_tok, w8.T, hp, x1, mod_of(l, 5), s13, s2, *norm_args,
                           out_dtype=BF16 if l + 1 < L else F32)
    return h
```
